```python
import math
import jax, jax.numpy as jnp
from jax import lax
import numpy as np

D_MODEL = 1024
BATCH = 2
SEQ = 16384
DEPTH = 2

CTX_LEN = 256
GRID_W = 64
EPS = 1e-6
HEAD_DIM = 64
ATTN_Q_HEADS = 8
ATTN_KV_HEADS = 2
ATTN_GROUP = ATTN_Q_HEADS // ATTN_KV_HEADS
ATTN_Q_DIM = ATTN_Q_HEADS * HEAD_DIM
ATTN_KV_DIM = ATTN_KV_HEADS * HEAD_DIM
WINDOW = 128
BLOCK = 128
ROPE_BASE = 10000.0
GDN_HEADS = 4
GDN_DK = 128
GDN_DV = 128
GDN_QK_DIM = GDN_HEADS * GDN_DK
GDN_V_DIM = GDN_HEADS * GDN_DV
GDN_CONV = 3
CHUNK = 64
HY_SPLITS = (ATTN_Q_DIM, ATTN_KV_DIM, ATTN_KV_DIM, 2 * GDN_QK_DIM + GDN_V_DIM, GDN_V_DIM, 2 * GDN_HEADS, 2 * GDN_HEADS)
HY_IN = sum(HY_SPLITS)
MIX_WIDTH = ATTN_Q_DIM + GDN_V_DIM
SC_CONV = 3
FFN_HIDDEN = -(-8 * D_MODEL // (3 * 256)) * 256

kernel_name = 'hybrid_swa_gdn_shortconv_dit_block'


def rms_norm(x, g):
    xf = x.astype(jnp.float32)
    y = xf * lax.rsqrt(jnp.mean(xf * xf, axis=-1, keepdims=True) + EPS)
    return (y * g.astype(jnp.float32)).astype(x.dtype)


def l2_normalize(x):
    return x * lax.rsqrt(jnp.sum(x * x, axis=-1, keepdims=True) + EPS)


def modulate(h, shift, scale):
    return h * (1 + scale) + shift


def ada_params(cond, w, b):
    return jnp.split(jax.nn.silu(cond) @ w + b, 6, axis=-1)


def centred_depthwise_conv(x, w):
    pad = w.shape[0] // 2
    return lax.conv_general_dilated(x, w[:, None, :].astype(x.dtype), window_strides=(1,), padding=[(pad, pad)],
                                    dimension_numbers=('NWC', 'WIO', 'NWC'), feature_group_count=x.shape[-1])


def axial_rope(length):
    rows = length // GRID_W
    row = jnp.broadcast_to(jnp.arange(rows)[:, None], (rows, GRID_W)).reshape(length).astype(jnp.float32)
    col = jnp.broadcast_to(jnp.arange(GRID_W)[None, :], (rows, GRID_W)).reshape(length).astype(jnp.float32)
    n_freq = HEAD_DIM // 4
    inv_freq = ROPE_BASE ** (-jnp.arange(n_freq, dtype=jnp.float32) / n_freq)
    ang = jnp.concatenate([row[:, None] * inv_freq, col[:, None] * inv_freq], axis=-1)
    return jnp.cos(ang), jnp.sin(ang)


def apply_rope(x, cos, sin):
    half = HEAD_DIM // 2
    xf = x.astype(jnp.float32)
    x1, x2 = xf[..., :half], xf[..., half:]
    cs, sn = cos[None, :, None, :], sin[None, :, None, :]
    return jnp.concatenate([x1 * cs - x2 * sn, x2 * cs + x1 * sn], axis=-1).astype(x.dtype)


def windowed_attention(q, k, v, k_ctx, v_ctx, sink_logit):
    B, L = q.shape[:2]
    n_blocks = L // BLOCK
    span = BLOCK + 2 * WINDOW
    scale = HEAD_DIM ** -0.5
    pad = ((0, 0), (WINDOW, WINDOW), (0, 0), (0, 0))
    kp, vp = jnp.pad(k, pad), jnp.pad(v, pad)
    rel = jnp.arange(span)[None, :] - WINDOW - jnp.arange(BLOCK)[:, None]
    in_band = jnp.abs(rel) <= WINDOW
    s_sink = jnp.broadcast_to(sink_logit[None, :, :, None, None], (B, ATTN_KV_HEADS, ATTN_GROUP, BLOCK, 1))

    def one_block(blk):
        start = blk * BLOCK
        qb = lax.dynamic_slice_in_dim(q, start, BLOCK, axis=1)
        kb = lax.dynamic_slice_in_dim(kp, start, span, axis=1)
        vb = lax.dynamic_slice_in_dim(vp, start, span, axis=1)
        key_pos = start - WINDOW + jnp.arange(span)
        valid = in_band & ((key_pos >= 0) & (key_pos < L))[None, :]
        s_loc = jnp.einsum('bnkgd,bmkd->bkgnm', qb, kb).astype(jnp.float32) * scale
        s_loc = jnp.where(valid, s_loc, -jnp.inf)
        s_ctx = jnp.einsum('bnkgd,bckd->bkgnc', qb, k_ctx).astype(jnp.float32) * scale
        p = jax.nn.softmax(jnp.concatenate([s_loc, s_ctx, s_sink], axis=-1), axis=-1).astype(v.dtype)
        return (jnp.einsum('bkgnm,bmkd->bnkgd', p[..., :span], vb)
                + jnp.einsum('bkgnc,bckd->bnkgd', p[..., span:-1], v_ctx))

    out = lax.map(one_block, jnp.arange(n_blocks))
    return jnp.moveaxis(out, 0, 1).reshape(B, L, ATTN_Q_DIM)


def context_attention(q_ctx, k_ctx, v_ctx, sink_logit):
    B, Lc = q_ctx.shape[:2]
    s = jnp.einsum('bnkgd,bckd->bkgnc', q_ctx, k_ctx).astype(jnp.float32) * HEAD_DIM ** -0.5
    s_sink = jnp.broadcast_to(sink_logit[None, :, :, None, None], s.shape[:-1] + (1,))
    p = jax.nn.softmax(jnp.concatenate([s, s_sink], axis=-1), axis=-1).astype(v_ctx.dtype)
    return jnp.einsum('bkgnc,bckd->bnkgd', p[..., :-1], v_ctx).reshape(B, Lc, ATTN_Q_DIM)


def gated_delta_chunked(q, k, v, g, beta, state0):
    B, H, L, dk = q.shape
    dv = v.shape[-1]
    n = L // CHUNK
    q = q.reshape(B, H, n, CHUNK, dk)
    k = k.reshape(B, H, n, CHUNK, dk)
    v = v.reshape(B, H, n, CHUNK, dv)
    g = jnp.cumsum(g.reshape(B, H, n, CHUNK), axis=-1)
    beta = beta.reshape(B, H, n, CHUNK)
    row = jnp.arange(CHUNK)[:, None]
    col = jnp.arange(CHUNK)[None, :]
    decay = jnp.exp(jnp.where(row >= col, g[..., :, None] - g[..., None, :], -jnp.inf))
    k_beta = k * beta[..., None]
    strict = jnp.where(row > col, jnp.einsum('bhncd,bhnsd->bhncs', k_beta, k) * decay, 0.0)
    eye = jnp.broadcast_to(jnp.eye(CHUNK, dtype=q.dtype), strict.shape)
    t_mat = lax.linalg.triangular_solve(eye + strict, eye, left_side=True, lower=True, unit_diagonal=True)
    u = jnp.einsum('bhncs,bhnse->bhnce', t_mat, v * beta[..., None])
    w = jnp.einsum('bhncs,bhnsd->bhncd', t_mat, k_beta * jnp.exp(g)[..., None])
    qk = jnp.einsum('bhncd,bhnsd->bhncs', q, k) * decay
    q_g = q * jnp.exp(g)[..., None]
    g_last = g[..., -1]
    k_tail = k * jnp.exp(g_last[..., None] - g)[..., None]

    def step(S, xs):
        u_c, w_c, qk_c, qg_c, kt_c, gl_c = xs
        v_new = u_c - jnp.einsum('bhcd,bhde->bhce', w_c, S)
        o_c = jnp.einsum('bhcd,bhde->bhce', qg_c, S) + jnp.einsum('bhcs,bhse->bhce', qk_c, v_new)
        S = S * jnp.exp(gl_c)[..., None, None] + jnp.einsum('bhcd,bhce->bhde', kt_c, v_new)
        return S, o_c

    xs = tuple(jnp.moveaxis(t, 2, 0) for t in (u, w, qk, q_g, k_tail, g_last))
    S, o = lax.scan(step, state0, xs)
    return jnp.moveaxis(o, 0, 2).reshape(B, H, L, dv), S


def gdn_prepare(qkv, a_proj, b_proj, conv_w, a_log, dt_bias):
    B, L, _ = qkv.shape
    qkv = jax.nn.silu(centred_depthwise_conv(qkv, conv_w)).astype(jnp.float32)
    q, k, v = jnp.split(qkv, [GDN_QK_DIM, 2 * GDN_QK_DIM], axis=-1)
    q = l2_normalize(q.reshape(B, L, GDN_HEADS, GDN_DK).transpose(0, 2, 1, 3)) * GDN_DK ** -0.5
    k = l2_normalize(k.reshape(B, L, GDN_HEADS, GDN_DK).transpose(0, 2, 1, 3))
    v = v.reshape(B, L, GDN_HEADS, GDN_DV).transpose(0, 2, 1, 3)
    a = a_proj.astype(jnp.float32).reshape(B, L, 2, GDN_HEADS)
    b = b_proj.astype(jnp.float32).reshape(B, L, 2, GDN_HEADS)
    g = -jnp.exp(a_log.astype(jnp.float32)) * jax.nn.softplus(a + dt_bias.astype(jnp.float32))
    beta = jax.nn.sigmoid(b)
    return q, k, v, g.transpose(2, 0, 3, 1), beta.transpose(2, 0, 3, 1)


def bidirectional_gdn(lat, ctx):
    q, k, v, g, beta = lat
    qc, kc, vc, gc, bc = ctx
    zero = jnp.zeros((q.shape[0], GDN_HEADS, GDN_DK, GDN_DV), jnp.float32)
    rev = lambda t: jnp.flip(t, axis=2)
    oc_f, s_f = gated_delta_chunked(qc, kc, vc, gc[0], bc[0], zero)
    o_f, _ = gated_delta_chunked(q, k, v, g[0], beta[0], s_f)
    oc_b, s_b = gated_delta_chunked(rev(qc), rev(kc), rev(vc), rev(gc[1]), rev(bc[1]), zero)
    o_b, _ = gated_delta_chunked(rev(q), rev(k), rev(v), rev(g[1]), rev(beta[1]), s_b)
    return o_f + rev(o_b), oc_f + rev(oc_b)


def gdn_output(o, z, norm_g):
    B, H, L, dv = o.shape
    o = rms_norm(o.transpose(0, 2, 1, 3), norm_g)
    gate = jax.nn.silu(z.astype(jnp.float32)).reshape(B, L, H, dv)
    return (o * gate).reshape(B, L, H * dv).astype(z.dtype)


def split_hybrid(p):
    idx = np.cumsum(HY_SPLITS)[:-1].tolist()
    return jnp.split(p, idx, axis=-1)


def attn_gdn_mixer(h, hc, w_in, w_out, sink, conv_w, a_log, dt_bias, norm_g, ctx_needed):
    B, L, _ = h.shape
    Lc = hc.shape[1]
    q, k, v, qkv, z, a, b = split_hybrid(h @ w_in)
    qc, kc, vc, qkv_c, z_c, a_c, b_c = split_hybrid(hc @ w_in)
    cos, sin = axial_rope(L)
    q = apply_rope(q.reshape(B, L, ATTN_Q_HEADS, HEAD_DIM), cos, sin).reshape(B, L, ATTN_KV_HEADS, ATTN_GROUP, HEAD_DIM)
    k = apply_rope(k.reshape(B, L, ATTN_KV_HEADS, HEAD_DIM), cos, sin)
    v = v.reshape(B, L, ATTN_KV_HEADS, HEAD_DIM)
    kc = kc.reshape(B, Lc, ATTN_KV_HEADS, HEAD_DIM)
    vc = vc.reshape(B, Lc, ATTN_KV_HEADS, HEAD_DIM)
    sink_logit = sink.reshape(ATTN_KV_HEADS, ATTN_GROUP).astype(jnp.float32)
    attn_out = windowed_attention(q, k, v, kc, vc, sink_logit)
    o, oc = bidirectional_gdn(gdn_prepare(qkv, a, b, conv_w, a_log, dt_bias),
                              gdn_prepare(qkv_c, a_c, b_c, conv_w, a_log, dt_bias))
    y = jnp.concatenate([attn_out, gdn_output(o, z, norm_g)], axis=-1) @ w_out
    if not ctx_needed:
        return y, None
    attn_c = context_attention(qc.reshape(B, Lc, ATTN_KV_HEADS, ATTN_GROUP, HEAD_DIM), kc, vc, sink_logit)
    yc = jnp.concatenate([attn_c, gdn_output(oc, z_c, norm_g)], axis=-1) @ w_out
    return y, yc


def short_conv_mixer(h, w_in, conv_w, w_out):
    b_gate, c_gate, u = jnp.split(h @ w_in, 3, axis=-1)
    return (b_gate * centred_depthwise_conv(c_gate * u, conv_w)) @ w_out


def swiglu(h, w_gate, w_up, w_down):
    return (jax.nn.silu(h @ w_gate) * (h @ w_up)) @ w_down


def setup_inputs(seed: int = 0) -> dict:
    key = jax.random.key(seed)
    ks = jax.random.split(key, 24)
    f32 = jnp.float32
    d = D_MODEL
    n_even = (DEPTH + 1) // 2
    n_odd = DEPTH // 2
    normal = lambda k, shape, s: jax.random.normal(k, shape, f32) * s
    gain = lambda k, shape: 1.0 + 0.02 * jax.random.normal(k, shape, f32)
    dt = jnp.exp(jax.random.uniform(ks[15], (n_even, 2, GDN_HEADS), f32, math.log(1e-3), math.log(1e-1)))
    return {
        'x': normal(ks[0], (BATCH, SEQ, d), 1.0),
        'c': normal(ks[1], (BATCH, d), 1.0),
        'ctx': normal(ks[2], (BATCH, CTX_LEN, d), 1.0),
        'c_ctx': normal(ks[3], (d,), 1.0),
        'ada_w': normal(ks[4], (DEPTH, d, 6 * d), 0.5 * d ** -0.5),
        'ada_b': normal(ks[5], (DEPTH, 6 * d), 0.01),
        'pre_mix_g': gain(ks[6], (DEPTH, d)),
        'post_mix_g': gain(ks[7], (DEPTH, d)),
        'pre_ffn_g': gain(ks[8], (DEPTH, d)),
        'post_ffn_g': gain(ks[9], (DEPTH, d)),
        'hy_w_in': normal(ks[10], (n_even, d, HY_IN), d ** -0.5),
        'hy_w_out': normal(ks[11], (n_even, MIX_WIDTH, d), MIX_WIDTH ** -0.5),
        'attn_sink': normal(ks[12], (n_even, ATTN_Q_HEADS), 1.0),
        'gdn_conv_w': normal(ks[13], (n_even, GDN_CONV, 2 * GDN_QK_DIM + GDN_V_DIM), GDN_CONV ** -0.5),
        'gdn_a_log': jnp.log(jax.random.uniform(ks[14], (n_even, 2, GDN_HEADS), f32, 1.0, 16.0)),
        'gdn_dt_bias': dt + jnp.log(-jnp.expm1(-dt)),
        'gdn_norm_g': gain(ks[16], (n_even, GDN_DV)),
        'sc_w_in': normal(ks[17], (n_odd, d, 3 * d), d ** -0.5),
        'sc_conv_w': normal(ks[18], (n_odd, SC_CONV, d), SC_CONV ** -0.5),
        'sc_w_out': normal(ks[19], (n_odd, d, d), d ** -0.5),
        'ffn_w_gate': normal(ks[20], (DEPTH, d, FFN_HIDDEN), d ** -0.5),
        'ffn_w_up': normal(ks[21], (DEPTH, d, FFN_HIDDEN), d ** -0.5),
        'ffn_w_down': normal(ks[22], (DEPTH, FFN_HIDDEN, d), FFN_HIDDEN ** -0.5),
    }


def reference(x, c, ctx, c_ctx, ada_w, ada_b, pre_mix_g, post_mix_g, pre_ffn_g, post_ffn_g,
              hy_w_in, hy_w_out, attn_sink, gdn_conv_w, gdn_a_log, gdn_dt_bias, gdn_norm_g,
              sc_w_in, sc_conv_w, sc_w_out, ffn_w_gate, ffn_w_up, ffn_w_down):
    for l in range(DEPTH):
        ctx_needed = any(j % 2 == 0 for j in range(l + 1, DEPTH))
        sh1, sc1, g1, sh2, sc2, g2 = ada_params(c, ada_w[l], ada_b[l])
        sh1, sc1, g1, sh2, sc2, g2 = (t[:, None, :] for t in (sh1, sc1, g1, sh2, sc2, g2))
        h = modulate(rms_norm(x, pre_mix_g[l]), sh1, sc1)
        if l % 2 == 0 or ctx_needed:
            csh1, csc1, cg1, csh2, csc2, cg2 = ada_params(c_ctx, ada_w[l], ada_b[l])
            hc = modulate(rms_norm(ctx, pre_mix_g[l]), csh1, csc1)
        if l % 2 == 0:
            e = l // 2
            y, yc = attn_gdn_mixer(h, hc, hy_w_in[e], hy_w_out[e], attn_sink[e], gdn_conv_w[e],
                                   gdn_a_log[e], gdn_dt_bias[e], gdn_norm_g[e], ctx_needed)
        else:
            o = l // 2
            y = short_conv_mixer(h, sc_w_in[o], sc_conv_w[o], sc_w_out[o])
            if ctx_needed:
                yc = short_conv_mixer(hc, sc_w_in[o], sc_conv_w[o], sc_w_out[o])
        x = x + g1 * rms_norm(y, post_mix_g[l])
        f = swiglu(modulate(rms_norm(x, pre_ffn_g[l]), sh2, sc2), ffn_w_gate[l], ffn_w_up[l], ffn_w_down[l])
        x = x + g2 * rms_norm(f, post_ffn_g[l])
        if ctx_needed:
            ctx = ctx + cg1 * rms_norm(yc, post_mix_g[l])
            fc = swiglu(modulate(rms_norm(ctx, pre_ffn_g[l]), csh2, csc2), ffn_w_gate[l], ffn_w_up[l], ffn_w_down[l])
            ctx = ctx + cg2 * rms_norm(fc, post_ffn_g[l])
    return x
```

```python
import functools

import jax
import jax.numpy as jnp
from jax import lax
from jax.experimental import pallas as pl
from jax.experimental.pallas import tpu as pltpu

F32 = jnp.float32
BF16 = jnp.bfloat16

EPS = 1e-6
GRID_W = 64
ROPE_BASE = 10000.0
HEAD_DIM = 64
ATTN_Q_HEADS = 8
ATTN_KV_HEADS = 2
ATTN_Q_DIM = ATTN_Q_HEADS * HEAD_DIM
ATTN_KV_DIM = ATTN_KV_HEADS * HEAD_DIM
WINDOW = 128
GDN_HEADS = 4
GDN_DK = 128
GDN_QK_DIM = GDN_HEADS * GDN_DK
GDN_V_DIM = GDN_HEADS * GDN_DK
GDN_QKV_DIM = 2 * GDN_QK_DIM + GDN_V_DIM
N_UNITS = 2 * GDN_HEADS
LANES = 128
SUBLANES = 8
CHUNK = 128
COL_Q = 0
COL_K = COL_Q + ATTN_Q_DIM
COL_V = COL_K + ATTN_KV_DIM
COL_QKV = COL_V + ATTN_KV_DIM
COL_Z = COL_QKV + GDN_QKV_DIM
COL_AB = COL_Z + GDN_V_DIM
HY_IN_PAD = COL_AB + LANES
KVM_WIDTH = 8 * LANES
VMEM_LIMIT = 56 * 1024 * 1024

_NT = (((1,), (1,)), ((), ()))


def _sigmoid(x):
    return 1.0 / (1.0 + jnp.exp(-x))


def _silu(x):
    return x * _sigmoid(x)


def _rms(x):
    return x * lax.rsqrt(jnp.mean(x * x, axis=-1, keepdims=True) + EPS)


def _dot(a, b):
    return jnp.dot(a.astype(BF16), b.astype(BF16), preferred_element_type=F32)


def _params(*sem):
    return pltpu.CompilerParams(dimension_semantics=sem, vmem_limit_bytes=VMEM_LIMIT)


def _resident(shape):
    zeros = (0,) * len(shape)
    return pl.BlockSpec(shape, lambda *_: zeros, pipeline_mode=pl.Buffered(1))


def _ada_kernel(cond_ref, w_ref, b_ref, out_ref):
    s = _silu(cond_ref[...])
    out_ref[0] = jnp.dot(s, w_ref[0], precision=lax.Precision.HIGHEST,
                         preferred_element_type=F32) + b_ref[0]


def _ada(cond, ada_w, ada_b, tn=1536):
    depth, d, n = ada_w.shape
    rows = cond.shape[0]
    return pl.pallas_call(
        _ada_kernel,
        grid=(depth, n // tn),
        in_specs=[pl.BlockSpec((rows, d), lambda l, j: (0, 0)),
                  pl.BlockSpec((1, d, tn), lambda l, j: (l, 0, j)),
                  pl.BlockSpec((1, 1, tn), lambda l, j: (l, 0, j))],
        out_specs=pl.BlockSpec((1, rows, tn), lambda l, j: (l, 0, j)),
        out_shape=jax.ShapeDtypeStruct((depth, rows, n), F32),
        compiler_params=_params("arbitrary", "arbitrary"),
        name="ada",
    )(cond, ada_w, ada_b.reshape(depth, 1, n))


def _modulated_norm(x, gain, mod, shift_row):
    return _rms(x) * gain * (1.0 + mod[shift_row + 1:shift_row + 2]) + mod[shift_row:shift_row + 1]


def _inproj_hy_kernel(x_ref, mod_ref, g_ref, w_ref, cos_ref, sin_ref,
                      q_out, kvm_out, qkv_out, z_out, ab_out):
    h = _modulated_norm(x_ref[0], g_ref[...], mod_ref[0], 0)
    p = _dot(h, w_ref[...])
    tm = p.shape[0]
    cos = cos_ref[...]
    sin = sin_ref[...]
    lane = lax.broadcasted_iota(jnp.int32, (tm, LANES), 1)
    first_half = (lane & (HEAD_DIM - 1)) < HEAD_DIM // 2
    left = lane < HEAD_DIM

    def rope(t):
        partner = jnp.where(first_half, pltpu.roll(t, LANES - HEAD_DIM // 2, 1),
                            pltpu.roll(t, HEAD_DIM // 2, 1))
        return t * cos + partner * sin

    for j in range(ATTN_Q_DIM // LANES):
        c0 = COL_Q + j * LANES
        q_out[0, :, j * LANES:(j + 1) * LANES] = (rope(p[:, c0:c0 + LANES]) * HEAD_DIM ** -0.5).astype(BF16)

    def masked_layouts(t):
        sw = pltpu.roll(t, HEAD_DIM, 1)
        return (jnp.where(left, t, 0.0), jnp.where(left, 0.0, sw),
                jnp.where(left, sw, 0.0), jnp.where(left, 0.0, t))

    k_lay = masked_layouts(rope(p[:, COL_K:COL_K + LANES]))
    v_lay = masked_layouts(p[:, COL_V:COL_V + LANES])
    for j, t in enumerate(k_lay + v_lay):
        kvm_out[0, :, j * LANES:(j + 1) * LANES] = t.astype(BF16)
    qkv_out[0] = p[:, COL_QKV:COL_Z]
    z_out[0] = p[:, COL_Z:COL_AB]
    ab_out[0] = p[:, COL_AB:HY_IN_PAD]


def _inproj_hy(x, mod, mod_row, gain, w_pad, cos, sin, tm):
    b, l, d = x.shape
    n = l // tm
    row_of = (lambda bi: bi) if mod_row is None else (lambda bi: mod_row)
    seq = lambda width: pl.BlockSpec((1, tm, width), lambda bi, i: (bi, i, 0))
    return pl.pallas_call(
        _inproj_hy_kernel,
        grid=(b, n),
        in_specs=[seq(d),
                  pl.BlockSpec((1, 6, d), lambda bi, i: (row_of(bi), 0, 0)),
                  _resident((1, d)),
                  _resident((d, HY_IN_PAD)),
                  pl.BlockSpec((tm, LANES), lambda bi, i: (i, 0)),
                  pl.BlockSpec((tm, LANES), lambda bi, i: (i, 0))],
        out_specs=[seq(ATTN_Q_DIM), seq(KVM_WIDTH), seq(GDN_QKV_DIM), seq(GDN_V_DIM), seq(LANES)],
        out_shape=[jax.ShapeDtypeStruct((b, l, ATTN_Q_DIM), BF16),
                   jax.ShapeDtypeStruct((b, l, KVM_WIDTH), BF16),
                   jax.ShapeDtypeStruct((b, l, GDN_QKV_DIM), F32),
                   jax.ShapeDtypeStruct((b, l, GDN_V_DIM), F32),
                   jax.ShapeDtypeStruct((b, l, LANES), F32)],
        compiler_params=_params("parallel", "arbitrary"),
        name="inproj_hy",
    )(x, mod, gain.reshape(1, d), w_pad, cos, sin)


def _conv3_rows(x, prev_row, next_row, w, row):
    tm = x.shape[0]
    xp = jnp.where(row == 0, prev_row, pltpu.roll(x, 1, 0))
    xn = jnp.where(row == tm - 1, next_row, pltpu.roll(x, tm - 1, 0))
    return w[0:1] * xp + w[1:2] * x + w[2:3] * xn


def _halo_specs(tm, width, n_rows):
    per = tm // SUBLANES
    last = n_rows // SUBLANES - 1
    prev = pl.BlockSpec((1, SUBLANES, width), lambda bi, i: (bi, jnp.maximum(i * per - 1, 0), 0))
    nxt = pl.BlockSpec((1, SUBLANES, width), lambda bi, i: (bi, jnp.minimum((i + 1) * per, last), 0))
    return prev, nxt


def _gdn_prep_kernel(x_ref, prev_ref, next_ref, ab_ref, cw_ref, par_ref,
                     q_out, k_out, v_out, gb_out, gr_out, *, n_tiles):
    i = pl.program_id(1)
    tm = x_ref.shape[1]
    row = lax.broadcasted_iota(jnp.int32, (tm, 1), 0)
    has_prev = i > 0
    has_next = i < n_tiles - 1
    for j in range(GDN_QKV_DIM // LANES):
        cols = slice(j * LANES, (j + 1) * LANES)
        prev_row = jnp.where(has_prev, prev_ref[0, SUBLANES - 1:SUBLANES, cols], 0.0)
        next_row = jnp.where(has_next, next_ref[0, 0:1, cols], 0.0)
        y = _silu(_conv3_rows(x_ref[0, :, cols], prev_row, next_row, cw_ref[:, cols], row))
        if j < GDN_HEADS:
            y = y * lax.rsqrt(jnp.sum(y * y, axis=-1, keepdims=True) + EPS) * GDN_DK ** -0.5
            q_out[0, :, cols] = y
        elif j < 2 * GDN_HEADS:
            y = y * lax.rsqrt(jnp.sum(y * y, axis=-1, keepdims=True) + EPS)
            k_out[0, :, j * LANES - GDN_QK_DIM:(j + 1) * LANES - GDN_QK_DIM] = y
        else:
            v_out[0, :, j * LANES - 2 * GDN_QK_DIM:(j + 1) * LANES - 2 * GDN_QK_DIM] = y

    ab = ab_ref[0]
    lane = lax.broadcasted_iota(jnp.int32, (tm, LANES), 1)
    t = ab + par_ref[1:2]
    softplus = jnp.maximum(t, 0.0) + jnp.log1p(jnp.exp(-jnp.abs(t)))
    g = -jnp.exp(par_ref[0:1]) * softplus
    beta = _sigmoid(ab)
    pos = row & (CHUNK - 1)
    pre = g
    suf = g
    s = 1
    while s < CHUNK:
        pre = pre + jnp.where(pos >= s, pltpu.roll(pre, s, 0), 0.0)
        suf = suf + jnp.where(pos < CHUNK - s, pltpu.roll(suf, tm - s, 0), 0.0)
        s *= 2
    gb = jnp.where(lane < GDN_HEADS, pre, jnp.where(lane < N_UNITS, suf, beta))
    gb_out[0] = gb
    gr_out[0] = gb.T[0:2 * N_UNITS]


def _gdn_prep(qkv, ab, conv_w, a_log, dt_bias, tm):
    b, l, _ = qkv.shape
    n = l // tm
    par = jnp.zeros((SUBLANES, LANES), F32)
    par = par.at[0, :N_UNITS].set(a_log.reshape(-1)).at[1, :N_UNITS].set(dt_bias.reshape(-1))
    prev, nxt = _halo_specs(tm, GDN_QKV_DIM, l)
    seq = lambda width: pl.BlockSpec((1, tm, width), lambda bi, i: (bi, i, 0))
    return pl.pallas_call(
        functools.partial(_gdn_prep_kernel, n_tiles=n),
        grid=(b, n),
        in_specs=[seq(GDN_QKV_DIM), prev, nxt, seq(LANES),
                  _resident((3, GDN_QKV_DIM)), _resident((SUBLANES, LANES))],
        out_specs=[seq(GDN_QK_DIM), seq(GDN_QK_DIM), seq(GDN_V_DIM), seq(LANES),
                   pl.BlockSpec((1, 2 * N_UNITS, tm), lambda bi, i: (bi, 0, i))],
        out_shape=[jax.ShapeDtypeStruct((b, l, GDN_QK_DIM), F32),
                   jax.ShapeDtypeStruct((b, l, GDN_QK_DIM), F32),
                   jax.ShapeDtypeStruct((b, l, GDN_V_DIM), F32),
                   jax.ShapeDtypeStruct((b, l, LANES), F32),
                   jax.ShapeDtypeStruct((b, 2 * N_UNITS, l), F32)],
        compiler_params=_params("parallel", "arbitrary"),
        name="gdn_prep",
    )(qkv, qkv, qkv, ab, conv_w, par)


def _unit_triangular_inverse(a, eye, diag_blocks, level_masks):
    ad = jnp.where(diag_blocks, a, 0.0)
    a2 = _dot(ad, ad)
    a4 = _dot(a2, a2)
    a8 = _dot(a4, a4)
    t = _dot(eye - ad, eye + a2)
    t = _dot(t, eye + a4)
    t = _dot(t, eye + a8)
    for m in level_masks:
        t = t - _dot(t, _dot(jnp.where(m, a, 0.0), t))
    return t


def _gdn_scan_kernel(qf_ref, kf_ref, vf_ref, gbf_ref, grf_ref,
                     qb_ref, kb_ref, vb_ref, gbb_ref, grb_ref, s0_ref,
                     of_ref, ob_ref, s_out, s_scr, *, n_blocks):
    i = pl.program_id(1)

    @pl.when(i == 0)
    def _():
        s_scr[...] = s0_ref[0]

    row = lax.broadcasted_iota(jnp.int32, (CHUNK, CHUNK), 0)
    col = lax.broadcasted_iota(jnp.int32, (CHUNK, CHUNK), 1)
    eye = jnp.where(row == col, 1.0, 0.0)
    same = lambda shift: (row >> shift) == (col >> shift)
    diag_blocks = same(4)
    level_masks = [same(sh) & jnp.logical_not(same(sh - 1)) for sh in (5, 6, 7)]

    directions = ((qf_ref, kf_ref, vf_ref, gbf_ref, grf_ref, of_ref),
                  (qb_ref, kb_ref, vb_ref, gbb_ref, grb_ref, ob_ref))
    for d, (q_ref, k_ref, v_ref, gb_ref, gr_ref, o_ref) in enumerate(directions):
        incl = (row >= col) if d == 0 else (row <= col)
        strict = (row > col) if d == 0 else (row < col)
        last = CHUNK - 1 if d == 0 else 0
        for h in range(GDN_HEADS):
            u = d * GDN_HEADS + h
            cols = slice(h * GDN_DK, (h + 1) * GDN_DK)
            q = q_ref[0, :, cols]
            k = k_ref[0, :, cols]
            v = v_ref[0, :, cols]
            g_col = gb_ref[0, :, u:u + 1]
            beta = gb_ref[0, :, N_UNITS + u:N_UNITS + u + 1]
            g_row = gr_ref[0, u:u + 1, :]
            g_last = g_col[last:last + 1, :]
            decay = jnp.exp(jnp.where(incl, g_col - g_row, -jnp.inf))
            kq = lax.dot_general(jnp.concatenate([k, q], axis=0).astype(BF16), k.astype(BF16), _NT,
                                 preferred_element_type=F32)
            a = jnp.where(strict, beta * kq[:CHUNK] * decay, 0.0)
            t_mat = _unit_triangular_inverse(a, eye, diag_blocks, level_masks)
            e_g = jnp.exp(g_col)
            state = s_scr[u]
            ps = _dot(jnp.concatenate([k * e_g, q * e_g], axis=0), state)
            v_new = _dot(t_mat, beta * (v - ps[:CHUNK]))
            o_ref[0, :, cols] = ps[CHUNK:] + _dot(kq[CHUNK:] * decay, v_new)
            k_tail = k * jnp.exp(g_last - g_col)
            s_scr[u] = state * jnp.exp(g_last) + _dot(k_tail.T, v_new)

    @pl.when(i == n_blocks - 1)
    def _():
        s_out[0] = s_scr[...]


def _gdn_scan(qn, kn, vv, gb, gr, state0):
    b, l, _ = qn.shape
    n = l // CHUNK
    fwd = lambda width: pl.BlockSpec((1, CHUNK, width), lambda bi, i: (bi, i, 0))
    bwd = lambda width: pl.BlockSpec((1, CHUNK, width), lambda bi, i: (bi, n - 1 - i, 0))
    gr_f = pl.BlockSpec((1, 2 * N_UNITS, CHUNK), lambda bi, i: (bi, 0, i))
    gr_b = pl.BlockSpec((1, 2 * N_UNITS, CHUNK), lambda bi, i: (bi, 0, n - 1 - i))
    st = pl.BlockSpec((1, N_UNITS, GDN_DK, GDN_DK), lambda bi, i: (bi, 0, 0, 0))
    return pl.pallas_call(
        functools.partial(_gdn_scan_kernel, n_blocks=n),
        grid=(b, n),
        in_specs=[fwd(GDN_QK_DIM), fwd(GDN_QK_DIM), fwd(GDN_V_DIM), fwd(LANES), gr_f,
                  bwd(GDN_QK_DIM), bwd(GDN_QK_DIM), bwd(GDN_V_DIM), bwd(LANES), gr_b, st],
        out_specs=[fwd(GDN_V_DIM), bwd(GDN_V_DIM), st],
        out_shape=[jax.ShapeDtypeStruct((b, l, GDN_V_DIM), F32),
                   jax.ShapeDtypeStruct((b, l, GDN_V_DIM), F32),
                   jax.ShapeDtypeStruct((b, N_UNITS, GDN_DK, GDN_DK), F32)],
        scratch_shapes=[pltpu.VMEM((N_UNITS, GDN_DK, GDN_DK), F32)],
        compiler_params=_params("parallel", "arbitrary"),
        name="gdn_scan",
    )(qn, kn, vv, gb, gr, qn, kn, vv, gb, gr, state0)


def _attn_kernel(q_ref, kvp_ref, kvc_ref, kvn_ref, ctx_ref, sink_ref, o_ref, *, n_blocks, n_ctx_tiles):
    i = pl.program_id(1)
    row = lax.broadcasted_iota(jnp.int32, (WINDOW, WINDOW), 0)
    col = lax.broadcasted_iota(jnp.int32, (WINDOW, WINDOW), 1)
    mask_prev = (col >= row) & (i > 0)
    mask_next = (col <= row) & (i < n_blocks - 1)
    for pair in range(ATTN_Q_HEADS // 2):
        kv_head = pair // (ATTN_Q_HEADS // ATTN_KV_HEADS // 2)
        q = q_ref[0, :, pair * LANES:(pair + 1) * LANES]
        acc = jnp.zeros((WINDOW, LANES), F32)
        for side in range(2):
            kc = (2 * kv_head + side) * LANES
            vc = 4 * LANES + kc
            sink = sink_ref[2 * pair + side]
            keys = [kvp_ref[0, :, kc:kc + LANES], kvc_ref[0, :, kc:kc + LANES], kvn_ref[0, :, kc:kc + LANES]]
            vals = [kvp_ref[0, :, vc:vc + LANES], kvc_ref[0, :, vc:vc + LANES], kvn_ref[0, :, vc:vc + LANES]]
            for t in range(n_ctx_tiles):
                keys.append(ctx_ref[0, t * WINDOW:(t + 1) * WINDOW, kc:kc + LANES])
                vals.append(ctx_ref[0, t * WINDOW:(t + 1) * WINDOW, vc:vc + LANES])
            s = [lax.dot_general(q, kt, _NT, preferred_element_type=F32) for kt in keys]
            s[0] = jnp.where(mask_prev, s[0], -jnp.inf)
            s[2] = jnp.where(mask_next, s[2], -jnp.inf)
            m = s[1]
            for st in s[:1] + s[2:]:
                m = jnp.maximum(m, st)
            m = jnp.maximum(jnp.max(m, axis=-1, keepdims=True), sink)
            e = [jnp.exp(st - m) for st in s]
            tot = e[0]
            for et in e[1:]:
                tot = tot + et
            denom = jnp.sum(tot, axis=-1, keepdims=True) + jnp.exp(sink - m)
            pv = jnp.dot(e[0].astype(BF16), vals[0], preferred_element_type=F32)
            for et, vt in zip(e[1:], vals[1:]):
                pv = pv + jnp.dot(et.astype(BF16), vt, preferred_element_type=F32)
            acc = acc + pv * (1.0 / denom)
        o_ref[0, :, pair * LANES:(pair + 1) * LANES] = acc.astype(BF16)


def _attn(q, kvm, kvm_ctx, sink):
    b, l, _ = q.shape
    lc = kvm_ctx.shape[1]
    n = l // WINDOW
    blk = lambda f: pl.BlockSpec((1, WINDOW, KVM_WIDTH), f)
    return pl.pallas_call(
        functools.partial(_attn_kernel, n_blocks=n, n_ctx_tiles=lc // WINDOW),
        grid=(b, n),
        in_specs=[pl.BlockSpec((1, WINDOW, ATTN_Q_DIM), lambda bi, i: (bi, i, 0)),
                  blk(lambda bi, i: (bi, jnp.maximum(i - 1, 0), 0)),
                  blk(lambda bi, i: (bi, i, 0)),
                  blk(lambda bi, i: (bi, jnp.minimum(i + 1, n - 1), 0)),
                  pl.BlockSpec((1, lc, KVM_WIDTH), lambda bi, i: (bi, 0, 0)),
                  pl.BlockSpec(memory_space=pltpu.SMEM)],
        out_specs=pl.BlockSpec((1, WINDOW, ATTN_Q_DIM), lambda bi, i: (bi, i, 0)),
        out_shape=jax.ShapeDtypeStruct((b, l, ATTN_Q_DIM), BF16),
        compiler_params=_params("parallel", "arbitrary"),
        name="attn",
    )(q, kvm, kvm, kvm, kvm_ctx, sink)


def _residual_ffn(x, y, mod, gains, wg_ref, wu_ref, wd_ref):
    x1 = x + mod[2:3] * (_rms(y) * gains[0:1])
    h = _modulated_norm(x1, gains[1:2], mod, 3).astype(BF16)
    gate = jnp.dot(h, wg_ref[...], preferred_element_type=F32)
    up = jnp.dot(h, wu_ref[...], preferred_element_type=F32)
    f = _dot(_silu(gate) * up, wd_ref[...])
    return x1 + mod[5:6] * (_rms(f) * gains[2:3])


def _post_hy_kernel(x_ref, attn_ref, of_ref, ob_ref, z_ref, mod_ref, gains_ref, ng_ref,
                    wo_ref, wg_ref, wu_ref, wd_ref, out_ref):
    o = of_ref[0] + ob_ref[0]
    z = z_ref[0]
    y = jnp.dot(attn_ref[0], wo_ref[0:ATTN_Q_DIM, :], preferred_element_type=F32)
    for h in range(GDN_HEADS):
        cols = slice(h * GDN_DK, (h + 1) * GDN_DK)
        gated = _rms(o[:, cols]) * ng_ref[...] * _silu(z[:, cols])
        y = y + _dot(gated, wo_ref[ATTN_Q_DIM + h * GDN_DK:ATTN_Q_DIM + (h + 1) * GDN_DK, :])
    out_ref[0] = _residual_ffn(x_ref[0], y, mod_ref[0], gains_ref[...], wg_ref, wu_ref, wd_ref)


def _post_hy(x, attn, o_f, o_b, z, mod, gains, norm_g, w_out, w_gate, w_up, w_down, tm):
    b, l, d = x.shape
    hidden = w_gate.shape[1]
    seq = lambda width: pl.BlockSpec((1, tm, width), lambda bi, i: (bi, i, 0))
    return pl.pallas_call(
        _post_hy_kernel,
        grid=(b, l // tm),
        in_specs=[seq(d), seq(ATTN_Q_DIM), seq(GDN_V_DIM), seq(GDN_V_DIM), seq(GDN_V_DIM),
                  pl.BlockSpec((1, 6, d), lambda bi, i: (bi, 0, 0)),
                  _resident((3, d)), _resident((1, GDN_DK)),
                  _resident((ATTN_Q_DIM + GDN_V_DIM, d)),
                  _resident((d, hidden)), _resident((d, hidden)), _resident((hidden, d))],
        out_specs=seq(d),
        out_shape=jax.ShapeDtypeStruct((b, l, d), F32),
        compiler_params=_params("parallel", "arbitrary"),
        name="post_hy",
    )(x, attn, o_f, o_b, z, mod, gains, norm_g.reshape(1, GDN_DK), w_out, w_gate, w_up, w_down)


def _inproj_sc_kernel(x_ref, mod_ref, g_ref, w_ref, b_out, cu_out):
    d = x_ref.shape[2]
    h = _modulated_norm(x_ref[0], g_ref[...], mod_ref[0], 0)
    p = _dot(h, w_ref[...])
    b_out[0] = p[:, 0:d]
    cu_out[0] = p[:, d:2 * d] * p[:, 2 * d:3 * d]


def _inproj_sc(x, mod, gain, w_in, tm):
    b, l, d = x.shape
    seq = pl.BlockSpec((1, tm, d), lambda bi, i: (bi, i, 0))
    return pl.pallas_call(
        _inproj_sc_kernel,
        grid=(b, l // tm),
        in_specs=[seq, pl.BlockSpec((1, 6, d), lambda bi, i: (bi, 0, 0)),
                  _resident((1, d)), _resident((d, 3 * d))],
        out_specs=[seq, seq],
        out_shape=[jax.ShapeDtypeStruct((b, l, d), F32), jax.ShapeDtypeStruct((b, l, d), F32)],
        compiler_params=_params("parallel", "arbitrary"),
        name="inproj_sc",
    )(x, mod, gain.reshape(1, d), w_in)


def _post_sc_kernel(x_ref, b_ref, cu_ref, prev_ref, next_ref, mod_ref, gains_ref, cw_ref,
                    wo_ref, wg_ref, wu_ref, wd_ref, out_ref, *, n_tiles):
    i = pl.program_id(1)
    tm = x_ref.shape[1]
    row = lax.broadcasted_iota(jnp.int32, (tm, 1), 0)
    prev_row = jnp.where(i > 0, prev_ref[0, SUBLANES - 1:SUBLANES, :], 0.0)
    next_row = jnp.where(i < n_tiles - 1, next_ref[0, 0:1, :], 0.0)
    conv = _conv3_rows(cu_ref[0], prev_row, next_row, cw_ref[...], row)
    y = _dot(b_ref[0] * conv, wo_ref[...])
    out_ref[0] = _residual_ffn(x_ref[0], y, mod_ref[0], gains_ref[...], wg_ref, wu_ref, wd_ref)


def _post_sc(x, b_gate, cu, mod, gains, conv_w, w_out, w_gate, w_up, w_down, tm):
    b, l, d = x.shape
    hidden = w_gate.shape[1]
    n = l // tm
    seq = pl.BlockSpec((1, tm, d), lambda bi, i: (bi, i, 0))
    prev, nxt = _halo_specs(tm, d, l)
    return pl.pallas_call(
        functools.partial(_post_sc_kernel, n_tiles=n),
        grid=(b, n),
        in_specs=[seq, seq, seq, prev, nxt,
                  pl.BlockSpec((1, 6, d), lambda bi, i: (bi, 0, 0)),
                  _resident((3, d)), _resident((3, d)), _resident((d, d)),
                  _resident((d, hidden)), _resident((d, hidden)), _resident((hidden, d))],
        out_specs=seq,
        out_shape=jax.ShapeDtypeStruct((b, l, d), F32),
        compiler_params=_params("parallel", "arbitrary"),
        name="post_sc",
    )(x, b_gate, cu, cu, cu, mod, gains, conv_w, w_out, w_gate, w_up, w_down)


def _rope_tables(length):
    pos = jnp.arange(length)
    n_freq = HEAD_DIM // 4
    inv_freq = ROPE_BASE ** (-jnp.arange(n_freq, dtype=F32) / n_freq)
    ang = jnp.concatenate([(pos // GRID_W).astype(F32)[:, None] * inv_freq,
                           (pos % GRID_W).astype(F32)[:, None] * inv_freq], axis=-1)
    cos, sin = jnp.cos(ang), jnp.sin(ang)
    return jnp.tile(cos, (1, 4)), jnp.tile(jnp.concatenate([-sin, sin], axis=-1), (1, 2))


def _row_tile(length, target):
    tm = min(length, target)
    assert length % tm == 0 and tm % CHUNK == 0
    return tm


def kernel(x, c, ctx, c_ctx, ada_w, ada_b, pre_mix_g, post_mix_g, pre_ffn_g, post_ffn_g, hy_w_in, hy_w_out,
           attn_sink, gdn_conv_w, gdn_a_log, gdn_dt_bias, gdn_norm_g, sc_w_in, sc_conv_w, sc_w_out,
           ffn_w_gate, ffn_w_up, ffn_w_down):
    b, l, d = x.shape
    lc = ctx.shape[1]
    assert ada_w.shape[0] == 2 and l % CHUNK == 0 and lc % CHUNK == 0 and b + 1 <= SUBLANES
    tm = _row_tile(l, 256)
    tmc = _row_tile(lc, 256)

    cond = jnp.concatenate([c, c_ctx[None], jnp.zeros((SUBLANES - b - 1, d), F32)], axis=0)
    mod = _ada(cond, ada_w, ada_b).reshape(2, SUBLANES, 6, d)
    gains = jnp.stack([post_mix_g, pre_ffn_g, post_ffn_g], axis=1)
    bf = lambda w: w.astype(BF16)

    w_in = bf(jnp.pad(hy_w_in[0], ((0, 0), (0, HY_IN_PAD - hy_w_in.shape[2]))))
    cos, sin = _rope_tables(l)
    q, kvm, qkv, z, ab = _inproj_hy(x, mod[0], None, pre_mix_g[0], w_in, cos, sin, tm)
    ones, zeros = jnp.ones((lc, LANES), F32), jnp.zeros((lc, LANES), F32)
    _, kvm_c, qkv_c, _, ab_c = _inproj_hy(ctx, mod[0], b, pre_mix_g[0], w_in, ones, zeros, tmc)

    lat = _gdn_prep(qkv, ab, gdn_conv_w[0], gdn_a_log[0], gdn_dt_bias[0], tm)
    con = _gdn_prep(qkv_c, ab_c, gdn_conv_w[0], gdn_a_log[0], gdn_dt_bias[0], tmc)
    state0 = jnp.zeros((b, N_UNITS, GDN_DK, GDN_DK), F32)
    _, _, state_c = _gdn_scan(*con, state0)
    o_f, o_b, _ = _gdn_scan(*lat, state_c)

    attn = _attn(q, kvm, kvm_c, attn_sink[0])
    x = _post_hy(x, attn, o_f, o_b, z, mod[0], gains[0], gdn_norm_g[0], bf(hy_w_out[0]),
                 bf(ffn_w_gate[0]), bf(ffn_w_up[0]), bf(ffn_w_down[0]), tm)

    b_gate, cu = _inproj_sc(x, mod[1], pre_mix_g[1], bf(sc_w_in[0]), tm)
    return _post_sc(x, b_gate, cu, mod[1], gains[1], sc_conv_w[0], bf(sc_w_out[0]),
                    bf(ffn_w_gate[1]), bf(ffn_w_up[1]), bf(ffn_w_down[1]), tm)
```

```python
import functools

import jax
import jax.numpy as jnp
from jax import lax
from jax.experimental import pallas as pl
from jax.experimental.pallas import tpu as pltpu

F32 = jnp.float32
BF16 = jnp.bfloat16

EPS = 1e-6
GRID_W = 64
ROPE_BASE = 10000.0
HEAD_DIM = 64
ATTN_Q_HEADS = 8
ATTN_KV_HEADS = 2
ATTN_Q_DIM = ATTN_Q_HEADS * HEAD_DIM
ATTN_KV_DIM = ATTN_KV_HEADS * HEAD_DIM
WINDOW = 128
GDN_HEADS = 4
GDN_DK = 128
GDN_QK_DIM = GDN_HEADS * GDN_DK
GDN_V_DIM = GDN_HEADS * GDN_DK
GDN_QKV_DIM = 2 * GDN_QK_DIM + GDN_V_DIM
N_UNITS = 2 * GDN_HEADS
LANES = 128
SUBLANES = 8
CHUNK = 128
COL_Q = 0
COL_K = COL_Q + ATTN_Q_DIM
COL_V = COL_K + ATTN_KV_DIM
COL_QKV = COL_V + ATTN_KV_DIM
COL_Z = COL_QKV + GDN_QKV_DIM
COL_AB = COL_Z + GDN_V_DIM
HY_IN_PAD = COL_AB + LANES
KVM_WIDTH = 8 * LANES
VMEM_LIMIT = 56 * 1024 * 1024

_NT = (((1,), (1,)), ((), ()))


def _sigmoid(x):
    return 1.0 / (1.0 + jnp.exp(-x))


def _silu(x):
    return x * _sigmoid(x)


def _rms(x):
    return x * lax.rsqrt(jnp.mean(x * x, axis=-1, keepdims=True) + EPS)


def _dot(a, b):
    return jnp.dot(a.astype(BF16), b.astype(BF16), preferred_element_type=F32)


def _params(*sem):
    return pltpu.CompilerParams(dimension_semantics=sem, vmem_limit_bytes=VMEM_LIMIT)


def _resident(shape):
    zeros = (0,) * len(shape)
    return pl.BlockSpec(shape, lambda *_: zeros, pipeline_mode=pl.Buffered(1))


def _ada_kernel(cond_ref, w_ref, b_ref, out_ref):
    s = _silu(cond_ref[...])
    out_ref[0] = jnp.dot(s, w_ref[0], precision=lax.Precision.HIGHEST,
                         preferred_element_type=F32) + b_ref[0]


def _ada(cond, ada_w, ada_b, tn=1536):
    depth, d, n = ada_w.shape
    rows = cond.shape[0]
    return pl.pallas_call(
        _ada_kernel,
        grid=(depth, n // tn),
        in_specs=[pl.BlockSpec((rows, d), lambda l, j: (0, 0)),
                  pl.BlockSpec((1, d, tn), lambda l, j: (l, 0, j)),
                  pl.BlockSpec((1, 1, tn), lambda l, j: (l, 0, j))],
        out_specs=pl.BlockSpec((1, rows, tn), lambda l, j: (l, 0, j)),
        out_shape=jax.ShapeDtypeStruct((depth, rows, n), F32),
        compiler_params=_params("arbitrary", "arbitrary"),
        name="ada",
    )(cond, ada_w, ada_b.reshape(depth, 1, n))


def _modulated_norm(x, gain, mod, shift_row):
    return _rms(x) * gain * (1.0 + mod[shift_row + 1:shift_row + 2]) + mod[shift_row:shift_row + 1]


def _inproj_hy_kernel(x_ref, mod_ref, g_ref, w_ref, cos_ref, sin_ref,
                      q_out, kvm_out, qkv_out, z_out, ab_out):
    h = _modulated_norm(x_ref[0], g_ref[...], mod_ref[0], 0)
    p = _dot(h, w_ref[...])
    tm = p.shape[0]
    cos = cos_ref[...]
    sin = sin_ref[...]
    lane = lax.broadcasted_iota(jnp.int32, (tm, LANES), 1)
    first_half = (lane & (HEAD_DIM - 1)) < HEAD_DIM // 2
    left = lane < HEAD_DIM

    def rope(t):
        partner = jnp.where(first_half, pltpu.roll(t, LANES - HEAD_DIM // 2, 1),
                            pltpu.roll(t, HEAD_DIM // 2, 1))
        return t * cos + partner * sin

    for j in range(ATTN_Q_DIM // LANES):
        c0 = COL_Q + j * LANES
        q_out[0, :, j * LANES:(j + 1) * LANES] = (rope(p[:, c0:c0 + LANES]) * HEAD_DIM ** -0.5).astype(BF16)

    def masked_layouts(t):
        sw = pltpu.roll(t, HEAD_DIM, 1)
        return (jnp.where(left, t, 0.0), jnp.where(left, 0.0, sw),
                jnp.where(left, sw, 0.0), jnp.where(left, 0.0, t))

    k_lay = masked_layouts(rope(p[:, COL_K:COL_K + LANES]))
    v_lay = masked_layouts(p[:, COL_V:COL_V + LANES])
    for j, t in enumerate(k_lay + v_lay):
        kvm_out[0, :, j * LANES:(j + 1) * LANES] = t.astype(BF16)
    qkv_out[0] = p[:, COL_QKV:COL_Z]
    z_out[0] = p[:, COL_Z:COL_AB]
    ab_out[0] = p[:, COL_AB:HY_IN_PAD]


def _inproj_hy(x, mod, mod_row, gain, w_pad, cos, sin, tm):
    b, l, d = x.shape
    n = l // tm
    row_of = (lambda bi: bi) if mod_row is None else (lambda bi: mod_row)
    seq = lambda width: pl.BlockSpec((1, tm, width), lambda bi, i: (bi, i, 0))
    return pl.pallas_call(
        _inproj_hy_kernel,
        grid=(b, n),
        in_specs=[seq(d),
                  pl.BlockSpec((1, 6, d), lambda bi, i: (row_of(bi), 0, 0)),
                  _resident((1, d)),
                  _resident((d, HY_IN_PAD)),
                  pl.BlockSpec((tm, LANES), lambda bi, i: (i, 0)),
                  pl.BlockSpec((tm, LANES), lambda bi, i: (i, 0))],
        out_specs=[seq(ATTN_Q_DIM), seq(KVM_WIDTH), seq(GDN_QKV_DIM), seq(GDN_V_DIM), seq(LANES)],
        out_shape=[jax.ShapeDtypeStruct((b, l, ATTN_Q_DIM), BF16),
                   jax.ShapeDtypeStruct((b, l, KVM_WIDTH), BF16),
                   jax.ShapeDtypeStruct((b, l, GDN_QKV_DIM), F32),
                   jax.ShapeDtypeStruct((b, l, GDN_V_DIM), F32),
                   jax.ShapeDtypeStruct((b, l, LANES), F32)],
        compiler_params=_params("parallel", "arbitrary"),
        name="inproj_hy",
    )(x, mod, gain.reshape(1, d), w_pad, cos, sin)


def _conv3_rows(x, prev_row, next_row, w, row):
    tm = x.shape[0]
    xp = jnp.where(row == 0, prev_row, pltpu.roll(x, 1, 0))
    xn = jnp.where(row == tm - 1, next_row, pltpu.roll(x, tm - 1, 0))
    return w[0:1] * xp + w[1:2] * x + w[2:3] * xn


def _halo_specs(tm, width, n_rows):
    per = tm // SUBLANES
    last = n_rows // SUBLANES - 1
    prev = pl.BlockSpec((1, SUBLANES, width), lambda bi, i: (bi, jnp.maximum(i * per - 1, 0), 0))
    nxt = pl.BlockSpec((1, SUBLANES, width), lambda bi, i: (bi, jnp.minimum((i + 1) * per, last), 0))
    return prev, nxt


def _gdn_prep_kernel(x_ref, prev_ref, next_ref, ab_ref, cw_ref, par_ref,
                     q_out, k_out, v_out, gb_out, gr_out, *, n_tiles):
    i = pl.program_id(1)
    tm = x_ref.shape[1]
    row = lax.broadcasted_iota(jnp.int32, (tm, 1), 0)
    has_prev = i > 0
    has_next = i < n_tiles - 1
    for j in range(GDN_QKV_DIM // LANES):
        cols = slice(j * LANES, (j + 1) * LANES)
        prev_row = jnp.where(has_prev, prev_ref[0, SUBLANES - 1:SUBLANES, cols], 0.0)
        next_row = jnp.where(has_next, next_ref[0, 0:1, cols], 0.0)
        y = _silu(_conv3_rows(x_ref[0, :, cols], prev_row, next_row, cw_ref[:, cols], row))
        if j < GDN_HEADS:
            y = y * lax.rsqrt(jnp.sum(y * y, axis=-1, keepdims=True) + EPS) * GDN_DK ** -0.5
            q_out[0, :, cols] = y
        elif j < 2 * GDN_HEADS:
            y = y * lax.rsqrt(jnp.sum(y * y, axis=-1, keepdims=True) + EPS)
            k_out[0, :, j * LANES - GDN_QK_DIM:(j + 1) * LANES - GDN_QK_DIM] = y
        else:
            v_out[0, :, j * LANES - 2 * GDN_QK_DIM:(j + 1) * LANES - 2 * GDN_QK_DIM] = y

    ab = ab_ref[0]
    lane = lax.broadcasted_iota(jnp.int32, (tm, LANES), 1)
    t = ab + par_ref[1:2]
    softplus = jnp.maximum(t, 0.0) + jnp.log1p(jnp.exp(-jnp.abs(t)))
    g = -jnp.exp(par_ref[0:1]) * softplus
    beta = _sigmoid(ab)
    pos = row & (CHUNK - 1)
    pre = g
    suf = g
    s = 1
    while s < CHUNK:
        pre = pre + jnp.where(pos >= s, pltpu.roll(pre, s, 0), 0.0)
        suf = suf + jnp.where(pos < CHUNK - s, pltpu.roll(suf, tm - s, 0), 0.0)
        s *= 2
    gb = jnp.where(lane < GDN_HEADS, pre, jnp.where(lane < N_UNITS, suf, beta))
    gb_out[0] = gb
    gr_out[0] = gb.T[0:2 * N_UNITS]


def _gdn_prep(qkv, ab, conv_w, a_log, dt_bias, tm):
    b, l, _ = qkv.shape
    n = l // tm
    par = jnp.zeros((SUBLANES, LANES), F32)
    par = par.at[0, :N_UNITS].set(a_log.reshape(-1)).at[1, :N_UNITS].set(dt_bias.reshape(-1))
    prev, nxt = _halo_specs(tm, GDN_QKV_DIM, l)
    seq = lambda width: pl.BlockSpec((1, tm, width), lambda bi, i: (bi, i, 0))
    return pl.pallas_call(
        functools.partial(_gdn_prep_kernel, n_tiles=n),
        grid=(b, n),
        in_specs=[seq(GDN_QKV_DIM), prev, nxt, seq(LANES),
                  _resident((3, GDN_QKV_DIM)), _resident((SUBLANES, LANES))],
        out_specs=[seq(GDN_QK_DIM), seq(GDN_QK_DIM), seq(GDN_V_DIM), seq(LANES),
                   pl.BlockSpec((1, 2 * N_UNITS, tm), lambda bi, i: (bi, 0, i))],
        out_shape=[jax.ShapeDtypeStruct((b, l, GDN_QK_DIM), F32),
                   jax.ShapeDtypeStruct((b, l, GDN_QK_DIM), F32),
                   jax.ShapeDtypeStruct((b, l, GDN_V_DIM), F32),
                   jax.ShapeDtypeStruct((b, l, LANES), F32),
                   jax.ShapeDtypeStruct((b, 2 * N_UNITS, l), F32)],
        compiler_params=_params("parallel", "arbitrary"),
        name="gdn_prep",
    )(qkv, qkv, qkv, ab, conv_w, par)


def _unit_triangular_inverses(a_list, eye, diag_blocks, level_masks):
    ad = [jnp.where(diag_blocks, a, 0.0) for a in a_list]
    a2 = [_dot(x, x) for x in ad]
    a4 = [_dot(x, x) for x in a2]
    t = [_dot(eye - x, eye + y) for x, y in zip(ad, a2)]
    a8 = [_dot(x, x) for x in a4]
    t = [_dot(x, eye + y) for x, y in zip(t, a4)]
    t = [_dot(x, eye + y) for x, y in zip(t, a8)]
    for m in level_masks:
        at = [_dot(jnp.where(m, a, 0.0), x) for a, x in zip(a_list, t)]
        t = [x - _dot(x, y) for x, y in zip(t, at)]
    return t


def _gdn_scan_kernel(qf_ref, kf_ref, vf_ref, gbf_ref, grf_ref,
                     qb_ref, kb_ref, vb_ref, gbb_ref, grb_ref, s0_ref,
                     of_ref, ob_ref, s_out, s_scr, *, n_blocks):
    i = pl.program_id(1)

    @pl.when(i == 0)
    def _():
        s_scr[...] = s0_ref[0]

    row = lax.broadcasted_iota(jnp.int32, (CHUNK, CHUNK), 0)
    col = lax.broadcasted_iota(jnp.int32, (CHUNK, CHUNK), 1)
    eye = jnp.where(row == col, 1.0, 0.0)
    same = lambda shift: (row >> shift) == (col >> shift)
    diag_blocks = same(4)
    level_masks = [same(sh) & jnp.logical_not(same(sh - 1)) for sh in (5, 6, 7)]

    units = []
    for d, refs in enumerate(((qf_ref, kf_ref, vf_ref, gbf_ref, grf_ref, of_ref),
                              (qb_ref, kb_ref, vb_ref, gbb_ref, grb_ref, ob_ref))):
        for h in range(GDN_HEADS):
            units.append((d, h, d * GDN_HEADS + h, slice(h * GDN_DK, (h + 1) * GDN_DK)) + refs)

    q, k, v, g_col, beta, g_last, decay, kq = [], [], [], [], [], [], [], []
    for d, h, u, cols, q_ref, k_ref, v_ref, gb_ref, gr_ref, o_ref in units:
        incl = (row >= col) if d == 0 else (row <= col)
        last = CHUNK - 1 if d == 0 else 0
        q.append(q_ref[0, :, cols])
        k.append(k_ref[0, :, cols])
        v.append(v_ref[0, :, cols])
        g_col.append(gb_ref[0, :, u:u + 1])
        beta.append(gb_ref[0, :, N_UNITS + u:N_UNITS + u + 1])
        g_last.append(g_col[-1][last:last + 1, :])
        decay.append(jnp.exp(jnp.where(incl, g_col[-1] - gr_ref[0, u:u + 1, :], -jnp.inf)))
        kq.append(lax.dot_general(jnp.concatenate([k[-1], q[-1]], axis=0).astype(BF16),
                                  k[-1].astype(BF16), _NT, preferred_element_type=F32))
    a = []
    for unit, kq_u, beta_u, decay_u in zip(units, kq, beta, decay):
        strict = (row > col) if unit[0] == 0 else (row < col)
        a.append(jnp.where(strict, beta_u * kq_u[:CHUNK] * decay_u, 0.0))
    t_mat = _unit_triangular_inverses(a, eye, diag_blocks, level_masks)

    state = [s_scr[u] for u in range(N_UNITS)]
    ps = [_dot(jnp.concatenate([k_u * jnp.exp(g_u), q_u * jnp.exp(g_u)], axis=0), s_u)
          for k_u, q_u, g_u, s_u in zip(k, q, g_col, state)]
    v_new = [_dot(t_u, beta_u * (v_u - ps_u[:CHUNK]))
             for t_u, beta_u, v_u, ps_u in zip(t_mat, beta, v, ps)]
    for unit, ps_u, kq_u, decay_u, v_new_u in zip(units, ps, kq, decay, v_new):
        cols, o_ref = unit[3], unit[9]
        o_ref[0, :, cols] = ps_u[CHUNK:] + _dot(kq_u[CHUNK:] * decay_u, v_new_u)
    for u, (k_u, g_u, gl_u, s_u, v_new_u) in enumerate(zip(k, g_col, g_last, state, v_new)):
        k_tail = k_u * jnp.exp(gl_u - g_u)
        s_scr[u] = s_u * jnp.exp(gl_u) + _dot(k_tail.T, v_new_u)

    @pl.when(i == n_blocks - 1)
    def _():
        s_out[0] = s_scr[...]


def _gdn_scan(qn, kn, vv, gb, gr, state0):
    b, l, _ = qn.shape
    n = l // CHUNK
    fwd = lambda width: pl.BlockSpec((1, CHUNK, width), lambda bi, i: (bi, i, 0))
    bwd = lambda width: pl.BlockSpec((1, CHUNK, width), lambda bi, i: (bi, n - 1 - i, 0))
    gr_f = pl.BlockSpec((1, 2 * N_UNITS, CHUNK), lambda bi, i: (bi, 0, i))
    gr_b = pl.BlockSpec((1, 2 * N_UNITS, CHUNK), lambda bi, i: (bi, 0, n - 1 - i))
    st = pl.BlockSpec((1, N_UNITS, GDN_DK, GDN_DK), lambda bi, i: (bi, 0, 0, 0))
    return pl.pallas_call(
        functools.partial(_gdn_scan_kernel, n_blocks=n),
        grid=(b, n),
        in_specs=[fwd(GDN_QK_DIM), fwd(GDN_QK_DIM), fwd(GDN_V_DIM), fwd(LANES), gr_f,
                  bwd(GDN_QK_DIM), bwd(GDN_QK_DIM), bwd(GDN_V_DIM), bwd(LANES), gr_b, st],
        out_specs=[fwd(GDN_V_DIM), bwd(GDN_V_DIM), st],
        out_shape=[jax.ShapeDtypeStruct((b, l, GDN_V_DIM), F32),
                   jax.ShapeDtypeStruct((b, l, GDN_V_DIM), F32),
                   jax.ShapeDtypeStruct((b, N_UNITS, GDN_DK, GDN_DK), F32)],
        scratch_shapes=[pltpu.VMEM((N_UNITS, GDN_DK, GDN_DK), F32)],
        compiler_params=_params("parallel", "arbitrary"),
        name="gdn_scan",
    )(qn, kn, vv, gb, gr, qn, kn, vv, gb, gr, state0)


def _attn_kernel(q_ref, kvp_ref, kvc_ref, kvn_ref, ctx_ref, sink_ref, o_ref, *, n_blocks, n_ctx_tiles):
    i = pl.program_id(1)
    row = lax.broadcasted_iota(jnp.int32, (WINDOW, WINDOW), 0)
    col = lax.broadcasted_iota(jnp.int32, (WINDOW, WINDOW), 1)
    mask_prev = (col >= row) & (i > 0)
    mask_next = (col <= row) & (i < n_blocks - 1)
    def tiles(head, base):
        kv_head = head // (ATTN_Q_HEADS // ATTN_KV_HEADS)
        c = (base + 2 * kv_head + head % 2) * LANES
        out = [kvp_ref[0, :, c:c + LANES], kvc_ref[0, :, c:c + LANES], kvn_ref[0, :, c:c + LANES]]
        return out + [ctx_ref[0, t * WINDOW:(t + 1) * WINDOW, c:c + LANES] for t in range(n_ctx_tiles)]

    heads = range(ATTN_Q_HEADS)
    scores = []
    for head in heads:
        q = q_ref[0, :, (head // 2) * LANES:(head // 2 + 1) * LANES]
        scores.append([lax.dot_general(q, kt, _NT, preferred_element_type=F32) for kt in tiles(head, 0)])
    probs, inv_denoms = [], []
    for head, s in zip(heads, scores):
        sink = sink_ref[head]
        s[0] = jnp.where(mask_prev, s[0], -jnp.inf)
        s[2] = jnp.where(mask_next, s[2], -jnp.inf)
        m = s[1]
        for st in s[:1] + s[2:]:
            m = jnp.maximum(m, st)
        m = jnp.maximum(jnp.max(m, axis=-1, keepdims=True), sink)
        e = [jnp.exp(st - m) for st in s]
        tot = e[0]
        for et in e[1:]:
            tot = tot + et
        inv_denoms.append(1.0 / (jnp.sum(tot, axis=-1, keepdims=True) + jnp.exp(sink - m)))
        probs.append([et.astype(BF16) for et in e])
    outs = []
    for head, e in zip(heads, probs):
        vals = tiles(head, 4)
        pv = jnp.dot(e[0], vals[0], preferred_element_type=F32)
        for et, vt in zip(e[1:], vals[1:]):
            pv = pv + jnp.dot(et, vt, preferred_element_type=F32)
        outs.append(pv)
    for pair in range(ATTN_Q_HEADS // 2):
        acc = outs[2 * pair] * inv_denoms[2 * pair] + outs[2 * pair + 1] * inv_denoms[2 * pair + 1]
        o_ref[0, :, pair * LANES:(pair + 1) * LANES] = acc.astype(BF16)


def _attn(q, kvm, kvm_ctx, sink):
    b, l, _ = q.shape
    lc = kvm_ctx.shape[1]
    n = l // WINDOW
    blk = lambda f: pl.BlockSpec((1, WINDOW, KVM_WIDTH), f)
    return pl.pallas_call(
        functools.partial(_attn_kernel, n_blocks=n, n_ctx_tiles=lc // WINDOW),
        grid=(b, n),
        in_specs=[pl.BlockSpec((1, WINDOW, ATTN_Q_DIM), lambda bi, i: (bi, i, 0)),
                  blk(lambda bi, i: (bi, jnp.maximum(i - 1, 0), 0)),
                  blk(lambda bi, i: (bi, i, 0)),
                  blk(lambda bi, i: (bi, jnp.minimum(i + 1, n - 1), 0)),
                  pl.BlockSpec((1, lc, KVM_WIDTH), lambda bi, i: (bi, 0, 0)),
                  pl.BlockSpec(memory_space=pltpu.SMEM)],
        out_specs=pl.BlockSpec((1, WINDOW, ATTN_Q_DIM), lambda bi, i: (bi, i, 0)),
        out_shape=jax.ShapeDtypeStruct((b, l, ATTN_Q_DIM), BF16),
        compiler_params=_params("parallel", "arbitrary"),
        name="attn",
    )(q, kvm, kvm, kvm, kvm_ctx, sink)


def _residual_ffn(x, y, mod, gains, wg_ref, wu_ref, wd_ref):
    x1 = x + mod[2:3] * (_rms(y) * gains[0:1])
    h = _modulated_norm(x1, gains[1:2], mod, 3).astype(BF16)
    gate = jnp.dot(h, wg_ref[...], preferred_element_type=F32)
    up = jnp.dot(h, wu_ref[...], preferred_element_type=F32)
    f = _dot(_silu(gate) * up, wd_ref[...])
    return x1 + mod[5:6] * (_rms(f) * gains[2:3])


def _post_hy_kernel(x_ref, attn_ref, of_ref, ob_ref, z_ref, mod_ref, gains_ref, ng_ref,
                    wo_ref, wg_ref, wu_ref, wd_ref, out_ref):
    o = of_ref[0] + ob_ref[0]
    z = z_ref[0]
    y = jnp.dot(attn_ref[0], wo_ref[0:ATTN_Q_DIM, :], preferred_element_type=F32)
    for h in range(GDN_HEADS):
        cols = slice(h * GDN_DK, (h + 1) * GDN_DK)
        gated = _rms(o[:, cols]) * ng_ref[...] * _silu(z[:, cols])
        y = y + _dot(gated, wo_ref[ATTN_Q_DIM + h * GDN_DK:ATTN_Q_DIM + (h + 1) * GDN_DK, :])
    out_ref[0] = _residual_ffn(x_ref[0], y, mod_ref[0], gains_ref[...], wg_ref, wu_ref, wd_ref)


def _post_hy(x, attn, o_f, o_b, z, mod, gains, norm_g, w_out, w_gate, w_up, w_down, tm):
    b, l, d = x.shape
    hidden = w_gate.shape[1]
    seq = lambda width: pl.BlockSpec((1, tm, width), lambda bi, i: (bi, i, 0))
    return pl.pallas_call(
        _post_hy_kernel,
        grid=(b, l // tm),
        in_specs=[seq(d), seq(ATTN_Q_DIM), seq(GDN_V_DIM), seq(GDN_V_DIM), seq(GDN_V_DIM),
                  pl.BlockSpec((1, 6, d), lambda bi, i: (bi, 0, 0)),
                  _resident((3, d)), _resident((1, GDN_DK)),
                  _resident((ATTN_Q_DIM + GDN_V_DIM, d)),
                  _resident((d, hidden)), _resident((d, hidden)), _resident((hidden, d))],
        out_specs=seq(d),
        out_shape=jax.ShapeDtypeStruct((b, l, d), F32),
        compiler_params=_params("parallel", "arbitrary"),
        name="post_hy",
    )(x, attn, o_f, o_b, z, mod, gains, norm_g.reshape(1, GDN_DK), w_out, w_gate, w_up, w_down)


def _inproj_sc_kernel(x_ref, mod_ref, g_ref, w_ref, b_out, cu_out):
    d = x_ref.shape[2]
    h = _modulated_norm(x_ref[0], g_ref[...], mod_ref[0], 0)
    p = _dot(h, w_ref[...])
    b_out[0] = p[:, 0:d]
    cu_out[0] = p[:, d:2 * d] * p[:, 2 * d:3 * d]


def _inproj_sc(x, mod, gain, w_in, tm):
    b, l, d = x.shape
    seq = pl.BlockSpec((1, tm, d), lambda bi, i: (bi, i, 0))
    return pl.pallas_call(
        _inproj_sc_kernel,
        grid=(b, l // tm),
        in_specs=[seq, pl.BlockSpec((1, 6, d), lambda bi, i: (bi, 0, 0)),
                  _resident((1, d)), _resident((d, 3 * d))],
        out_specs=[seq, seq],
        out_shape=[jax.ShapeDtypeStruct((b, l, d), F32), jax.ShapeDtypeStruct((b, l, d), F32)],
        compiler_params=_params("parallel", "arbitrary"),
        name="inproj_sc",
    )(x, mod, gain.reshape(1, d), w_in)


def _post_sc_kernel(x_ref, b_ref, cu_ref, prev_ref, next_ref, mod_ref, gains_ref, cw_ref,
                    wo_ref, wg_ref, wu_ref, wd_ref, out_ref, *, n_tiles):
    i = pl.program_id(1)
    tm = x_ref.shape[1]
    row = lax.broadcasted_iota(jnp.int32, (tm, 1), 0)
    prev_row = jnp.where(i > 0, prev_ref[0, SUBLANES - 1:SUBLANES, :], 0.0)
    next_row = jnp.where(i < n_tiles - 1, next_ref[0, 0:1, :], 0.0)
    conv = _conv3_rows(cu_ref[0], prev_row, next_row, cw_ref[...], row)
    y = _dot(b_ref[0] * conv, wo_ref[...])
    out_ref[0] = _residual_ffn(x_ref[0], y, mod_ref[0], gains_ref[...], wg_ref, wu_ref, wd_ref)


def _post_sc(x, b_gate, cu, mod, gains, conv_w, w_out, w_gate, w_up, w_down, tm):
    b, l, d = x.shape
    hidden = w_gate.shape[1]
    n = l // tm
    seq = pl.BlockSpec((1, tm, d), lambda bi, i: (bi, i, 0))
    prev, nxt = _halo_specs(tm, d, l)
    return pl.pallas_call(
        functools.partial(_post_sc_kernel, n_tiles=n),
        grid=(b, n),
        in_specs=[seq, seq, seq, prev, nxt,
                  pl.BlockSpec((1, 6, d), lambda bi, i: (bi, 0, 0)),
                  _resident((3, d)), _resident((3, d)), _resident((d, d)),
                  _resident((d, hidden)), _resident((d, hidden)), _resident((hidden, d))],
        out_specs=seq,
        out_shape=jax.ShapeDtypeStruct((b, l, d), F32),
        compiler_params=_params("parallel", "arbitrary"),
        name="post_sc",
    )(x, b_gate, cu, cu, cu, mod, gains, conv_w, w_out, w_gate, w_up, w_down)


def _rope_tables(length):
    pos = jnp.arange(length)
    n_freq = HEAD_DIM // 4
    inv_freq = ROPE_BASE ** (-jnp.arange(n_freq, dtype=F32) / n_freq)
    ang = jnp.concatenate([(pos // GRID_W).astype(F32)[:, None] * inv_freq,
                           (pos % GRID_W).astype(F32)[:, None] * inv_freq], axis=-1)
    cos, sin = jnp.cos(ang), jnp.sin(ang)
    return jnp.tile(cos, (1, 4)), jnp.tile(jnp.concatenate([-sin, sin], axis=-1), (1, 2))


def _row_tile(length, target):
    tm = min(length, target)
    assert length % tm == 0 and tm % CHUNK == 0
    return tm


def kernel(x, c, ctx, c_ctx, ada_w, ada_b, pre_mix_g, post_mix_g, pre_ffn_g, post_ffn_g, hy_w_in, hy_w_out,
           attn_sink, gdn_conv_w, gdn_a_log, gdn_dt_bias, gdn_norm_g, sc_w_in, sc_conv_w, sc_w_out,
           ffn_w_gate, ffn_w_up, ffn_w_down):
    b, l, d = x.shape
    lc = ctx.shape[1]
    assert ada_w.shape[0] == 2 and l % CHUNK == 0 and lc % CHUNK == 0 and b + 1 <= SUBLANES
    tm = _row_tile(l, 256)
    tmc = _row_tile(lc, 256)

    cond = jnp.concatenate([c, c_ctx[None], jnp.zeros((SUBLANES - b - 1, d), F32)], axis=0)
    mod = _ada(cond, ada_w, ada_b).reshape(2, SUBLANES, 6, d)
    gains = jnp.stack([post_mix_g, pre_ffn_g, post_ffn_g], axis=1)
    bf = lambda w: w.astype(BF16)

    w_in = bf(jnp.pad(hy_w_in[0], ((0, 0), (0, HY_IN_PAD - hy_w_in.shape[2]))))
    cos, sin = _rope_tables(l)
    q, kvm, qkv, z, ab = _inproj_hy(x, mod[0], None, pre_mix_g[0], w_in, cos, sin, tm)
    ones, zeros = jnp.ones((lc, LANES), F32), jnp.zeros((lc, LANES), F32)
    _, kvm_c, qkv_c, _, ab_c = _inproj_hy(ctx, mod[0], b, pre_mix_g[0], w_in, ones, zeros, tmc)

    lat = _gdn_prep(qkv, ab, gdn_conv_w[0], gdn_a_log[0], gdn_dt_bias[0], tm)
    con = _gdn_prep(qkv_c, ab_c, gdn_conv_w[0], gdn_a_log[0], gdn_dt_bias[0], tmc)
    state0 = jnp.zeros((b, N_UNITS, GDN_DK, GDN_DK), F32)
    _, _, state_c = _gdn_scan(*con, state0)
    o_f, o_b, _ = _gdn_scan(*lat, state_c)

    attn = _attn(q, kvm, kvm_c, attn_sink[0])
    x = _post_hy(x, attn, o_f, o_b, z, mod[0], gains[0], gdn_norm_g[0], bf(hy_w_out[0]),
                 bf(ffn_w_gate[0]), bf(ffn_w_up[0]), bf(ffn_w_down[0]), tm)

    b_gate, cu = _inproj_sc(x, mod[1], pre_mix_g[1], bf(sc_w_in[0]), tm)
    return _post_sc(x, b_gate, cu, mod[1], gains[1], sc_conv_w[0], bf(sc_w_out[0]),
                    bf(ffn_w_gate[1]), bf(ffn_w_up[1]), bf(ffn_w_down[1]), tm)
```

```python
import functools

import jax
import jax.numpy as jnp
from jax import lax
from jax.experimental import pallas as pl
from jax.experimental.pallas import tpu as pltpu

F32 = jnp.float32
BF16 = jnp.bfloat16

EPS = 1e-6
GRID_W = 64
ROPE_BASE = 10000.0
HEAD_DIM = 64
ATTN_Q_HEADS = 8
ATTN_KV_HEADS = 2
ATTN_Q_DIM = ATTN_Q_HEADS * HEAD_DIM
ATTN_KV_DIM = ATTN_KV_HEADS * HEAD_DIM
WINDOW = 128
GDN_HEADS = 4
GDN_DK = 128
GDN_QK_DIM = GDN_HEADS * GDN_DK
GDN_V_DIM = GDN_HEADS * GDN_DK
GDN_QKV_DIM = 2 * GDN_QK_DIM + GDN_V_DIM
N_UNITS = 2 * GDN_HEADS
LANES = 128
SUBLANES = 8
CHUNK = 128
COL_Q = 0
COL_K = COL_Q + ATTN_Q_DIM
COL_V = COL_K + ATTN_KV_DIM
COL_QKV = COL_V + ATTN_KV_DIM
COL_Z = COL_QKV + GDN_QKV_DIM
COL_AB = COL_Z + GDN_V_DIM
HY_IN_PAD = COL_AB + LANES
KVM_WIDTH = 8 * LANES
VMEM_LIMIT = 56 * 1024 * 1024

_NT = (((1,), (1,)), ((), ()))


def _sigmoid(x):
    return 1.0 / (1.0 + jnp.exp(-x))


def _silu(x):
    return x * _sigmoid(x)


def _rms(x):
    return x * lax.rsqrt(jnp.mean(x * x, axis=-1, keepdims=True) + EPS)


def _dot(a, b):
    return jnp.dot(a.astype(BF16), b.astype(BF16), preferred_element_type=F32)


def _params(*sem):
    return pltpu.CompilerParams(dimension_semantics=sem, vmem_limit_bytes=VMEM_LIMIT)


def _resident(shape):
    zeros = (0,) * len(shape)
    return pl.BlockSpec(shape, lambda *_: zeros, pipeline_mode=pl.Buffered(1))


def _ada_kernel(cond_ref, w_ref, b_ref, out_ref):
    s = _silu(cond_ref[...])
    out_ref[0] = jnp.dot(s, w_ref[0], precision=lax.Precision.HIGHEST,
                         preferred_element_type=F32) + b_ref[0]


def _ada(cond, ada_w, ada_b, tn=1536):
    depth, d, n = ada_w.shape
    rows = cond.shape[0]
    return pl.pallas_call(
        _ada_kernel,
        grid=(depth, n // tn),
        in_specs=[pl.BlockSpec((rows, d), lambda l, j: (0, 0)),
                  pl.BlockSpec((1, d, tn), lambda l, j: (l, 0, j)),
                  pl.BlockSpec((1, 1, tn), lambda l, j: (l, 0, j))],
        out_specs=pl.BlockSpec((1, rows, tn), lambda l, j: (l, 0, j)),
        out_shape=jax.ShapeDtypeStruct((depth, rows, n), F32),
        compiler_params=_params("arbitrary", "arbitrary"),
        name="ada",
    )(cond, ada_w, ada_b.reshape(depth, 1, n))


def _modulated_norm(x, gain, mod, shift_row):
    return _rms(x) * gain * (1.0 + mod[shift_row + 1:shift_row + 2]) + mod[shift_row:shift_row + 1]


def _inproj_hy_kernel(x_ref, mod_ref, g_ref, w_ref, cos_ref, sin_ref,
                      q_out, kvm_out, qkv_out, z_out, ab_out):
    h = _modulated_norm(x_ref[0], g_ref[...], mod_ref[0], 0)
    p = _dot(h, w_ref[...])
    tm = p.shape[0]
    cos = cos_ref[...]
    sin = sin_ref[...]
    lane = lax.broadcasted_iota(jnp.int32, (tm, LANES), 1)
    first_half = (lane & (HEAD_DIM - 1)) < HEAD_DIM // 2
    left = lane < HEAD_DIM

    def rope(t):
        partner = jnp.where(first_half, pltpu.roll(t, LANES - HEAD_DIM // 2, 1),
                            pltpu.roll(t, HEAD_DIM // 2, 1))
        return t * cos + partner * sin

    for j in range(ATTN_Q_DIM // LANES):
        c0 = COL_Q + j * LANES
        q_out[0, :, j * LANES:(j + 1) * LANES] = (rope(p[:, c0:c0 + LANES]) * HEAD_DIM ** -0.5).astype(BF16)

    def masked_layouts(t):
        sw = pltpu.roll(t, HEAD_DIM, 1)
        return (jnp.where(left, t, 0.0), jnp.where(left, 0.0, sw),
                jnp.where(left, sw, 0.0), jnp.where(left, 0.0, t))

    k_lay = masked_layouts(rope(p[:, COL_K:COL_K + LANES]))
    v_lay = masked_layouts(p[:, COL_V:COL_V + LANES])
    for j, t in enumerate(k_lay + v_lay):
        kvm_out[0, :, j * LANES:(j + 1) * LANES] = t.astype(BF16)
    qkv_out[0] = p[:, COL_QKV:COL_Z]
    z_out[0] = p[:, COL_Z:COL_AB]
    ab_out[0] = p[:, COL_AB:HY_IN_PAD]


def _inproj_hy(x, mod, mod_row, gain, w_pad, cos, sin, tm):
    b, l, d = x.shape
    n = l // tm
    row_of = (lambda bi: bi) if mod_row is None else (lambda bi: mod_row)
    seq = lambda width: pl.BlockSpec((1, tm, width), lambda bi, i: (bi, i, 0))
    return pl.pallas_call(
        _inproj_hy_kernel,
        grid=(b, n),
        in_specs=[seq(d),
                  pl.BlockSpec((1, 6, d), lambda bi, i: (row_of(bi), 0, 0)),
                  _resident((1, d)),
                  _resident((d, HY_IN_PAD)),
                  pl.BlockSpec((tm, LANES), lambda bi, i: (i, 0)),
                  pl.BlockSpec((tm, LANES), lambda bi, i: (i, 0))],
        out_specs=[seq(ATTN_Q_DIM), seq(KVM_WIDTH), seq(GDN_QKV_DIM), seq(GDN_V_DIM), seq(LANES)],
        out_shape=[jax.ShapeDtypeStruct((b, l, ATTN_Q_DIM), BF16),
                   jax.ShapeDtypeStruct((b, l, KVM_WIDTH), BF16),
                   jax.ShapeDtypeStruct((b, l, GDN_QKV_DIM), F32),
                   jax.ShapeDtypeStruct((b, l, GDN_V_DIM), F32),
                   jax.ShapeDtypeStruct((b, l, LANES), F32)],
        compiler_params=_params("parallel", "arbitrary"),
        name="inproj_hy",
    )(x, mod, gain.reshape(1, d), w_pad, cos, sin)


def _conv3_rows(x, prev_row, next_row, w, row):
    tm = x.shape[0]
    xp = jnp.where(row == 0, prev_row, pltpu.roll(x, 1, 0))
    xn = jnp.where(row == tm - 1, next_row, pltpu.roll(x, tm - 1, 0))
    return w[0:1] * xp + w[1:2] * x + w[2:3] * xn


def _halo_specs(tm, width, n_rows):
    per = tm // SUBLANES
    last = n_rows // SUBLANES - 1
    prev = pl.BlockSpec((1, SUBLANES, width), lambda bi, i: (bi, jnp.maximum(i * per - 1, 0), 0))
    nxt = pl.BlockSpec((1, SUBLANES, width), lambda bi, i: (bi, jnp.minimum((i + 1) * per, last), 0))
    return prev, nxt


def _gdn_prep_kernel(x_ref, prev_ref, next_ref, ab_ref, cw_ref, par_ref,
                     q_out, k_out, v_out, gb_out, gr_out, *, n_tiles):
    i = pl.program_id(1)
    tm = x_ref.shape[1]
    row = lax.broadcasted_iota(jnp.int32, (tm, 1), 0)
    has_prev = i > 0
    has_next = i < n_tiles - 1
    for j in range(GDN_QKV_DIM // LANES):
        cols = slice(j * LANES, (j + 1) * LANES)
        prev_row = jnp.where(has_prev, prev_ref[0, SUBLANES - 1:SUBLANES, cols], 0.0)
        next_row = jnp.where(has_next, next_ref[0, 0:1, cols], 0.0)
        y = _silu(_conv3_rows(x_ref[0, :, cols], prev_row, next_row, cw_ref[:, cols], row))
        if j < GDN_HEADS:
            y = y * lax.rsqrt(jnp.sum(y * y, axis=-1, keepdims=True) + EPS) * GDN_DK ** -0.5
            q_out[0, :, cols] = y
        elif j < 2 * GDN_HEADS:
            y = y * lax.rsqrt(jnp.sum(y * y, axis=-1, keepdims=True) + EPS)
            k_out[0, :, j * LANES - GDN_QK_DIM:(j + 1) * LANES - GDN_QK_DIM] = y
        else:
            v_out[0, :, j * LANES - 2 * GDN_QK_DIM:(j + 1) * LANES - 2 * GDN_QK_DIM] = y

    ab = ab_ref[0]
    lane = lax.broadcasted_iota(jnp.int32, (tm, LANES), 1)
    t = ab + par_ref[1:2]
    softplus = jnp.maximum(t, 0.0) + jnp.log1p(jnp.exp(-jnp.abs(t)))
    g = -jnp.exp(par_ref[0:1]) * softplus
    beta = _sigmoid(ab)
    pos = row & (CHUNK - 1)
    pre = g
    suf = g
    s = 1
    while s < CHUNK:
        pre = pre + jnp.where(pos >= s, pltpu.roll(pre, s, 0), 0.0)
        suf = suf + jnp.where(pos < CHUNK - s, pltpu.roll(suf, tm - s, 0), 0.0)
        s *= 2
    gb = jnp.where(lane < GDN_HEADS, pre, jnp.where(lane < N_UNITS, suf, beta))
    gb_out[0] = gb
    gr_out[0] = gb.T[0:2 * N_UNITS]


def _gdn_prep(qkv, ab, conv_w, a_log, dt_bias, tm):
    b, l, _ = qkv.shape
    n = l // tm
    par = jnp.zeros((SUBLANES, LANES), F32)
    par = par.at[0, :N_UNITS].set(a_log.reshape(-1)).at[1, :N_UNITS].set(dt_bias.reshape(-1))
    prev, nxt = _halo_specs(tm, GDN_QKV_DIM, l)
    seq = lambda width: pl.BlockSpec((1, tm, width), lambda bi, i: (bi, i, 0))
    return pl.pallas_call(
        functools.partial(_gdn_prep_kernel, n_tiles=n),
        grid=(b, n),
        in_specs=[seq(GDN_QKV_DIM), prev, nxt, seq(LANES),
                  _resident((3, GDN_QKV_DIM)), _resident((SUBLANES, LANES))],
        out_specs=[seq(GDN_QK_DIM), seq(GDN_QK_DIM), seq(GDN_V_DIM), seq(LANES),
                   pl.BlockSpec((1, 2 * N_UNITS, tm), lambda bi, i: (bi, 0, i))],
        out_shape=[jax.ShapeDtypeStruct((b, l, GDN_QK_DIM), F32),
                   jax.ShapeDtypeStruct((b, l, GDN_QK_DIM), F32),
                   jax.ShapeDtypeStruct((b, l, GDN_V_DIM), F32),
                   jax.ShapeDtypeStruct((b, l, LANES), F32),
                   jax.ShapeDtypeStruct((b, 2 * N_UNITS, l), F32)],
        compiler_params=_params("parallel", "arbitrary"),
        name="gdn_prep",
    )(qkv, qkv, qkv, ab, conv_w, par)


def _unit_triangular_inverses(a_list, eye, diag_blocks, level_masks):
    ad = [jnp.where(diag_blocks, a, 0.0) for a in a_list]
    a2 = [_dot(x, x) for x in ad]
    a4 = [_dot(x, x) for x in a2]
    t = [_dot(eye - x, eye + y) for x, y in zip(ad, a2)]
    a8 = [_dot(x, x) for x in a4]
    t = [_dot(x, eye + y) for x, y in zip(t, a4)]
    t = [_dot(x, eye + y) for x, y in zip(t, a8)]
    for m in level_masks:
        at = [_dot(jnp.where(m, a, 0.0), x) for a, x in zip(a_list, t)]
        t = [x - _dot(x, y) for x, y in zip(t, at)]
    return t


def _gdn_scan_kernel(qf_ref, kf_ref, vf_ref, gbf_ref, grf_ref,
                     qb_ref, kb_ref, vb_ref, gbb_ref, grb_ref, s0_ref,
                     of_ref, ob_ref, s_out, s_scr, *, n_blocks):
    i = pl.program_id(1)

    @pl.when(i == 0)
    def _():
        s_scr[...] = s0_ref[0]

    row = lax.broadcasted_iota(jnp.int32, (CHUNK, CHUNK), 0)
    col = lax.broadcasted_iota(jnp.int32, (CHUNK, CHUNK), 1)
    eye = jnp.where(row == col, 1.0, 0.0)
    same = lambda shift: (row >> shift) == (col >> shift)
    diag_blocks = same(4)
    level_masks = [same(sh) & jnp.logical_not(same(sh - 1)) for sh in (5, 6, 7)]

    units = []
    for d, refs in enumerate(((qf_ref, kf_ref, vf_ref, gbf_ref, grf_ref, of_ref),
                              (qb_ref, kb_ref, vb_ref, gbb_ref, grb_ref, ob_ref))):
        for h in range(GDN_HEADS):
            units.append((d, h, d * GDN_HEADS + h, slice(h * GDN_DK, (h + 1) * GDN_DK)) + refs)

    q, k, v, g_col, beta, g_last, decay, kq = [], [], [], [], [], [], [], []
    for d, h, u, cols, q_ref, k_ref, v_ref, gb_ref, gr_ref, o_ref in units:
        incl = (row >= col) if d == 0 else (row <= col)
        last = CHUNK - 1 if d == 0 else 0
        q.append(q_ref[0, :, cols])
        k.append(k_ref[0, :, cols])
        v.append(v_ref[0, :, cols])
        g_col.append(gb_ref[0, :, u:u + 1])
        beta.append(gb_ref[0, :, N_UNITS + u:N_UNITS + u + 1])
        g_last.append(g_col[-1][last:last + 1, :])
        decay.append(jnp.exp(jnp.where(incl, g_col[-1] - gr_ref[0, u:u + 1, :], -jnp.inf)))
        kq.append(lax.dot_general(jnp.concatenate([k[-1], q[-1]], axis=0).astype(BF16),
                                  k[-1].astype(BF16), _NT, preferred_element_type=F32))
    a = []
    for unit, kq_u, beta_u, decay_u in zip(units, kq, beta, decay):
        strict = (row > col) if unit[0] == 0 else (row < col)
        a.append(jnp.where(strict, beta_u * kq_u[:CHUNK] * decay_u, 0.0))
    t_mat = _unit_triangular_inverses(a, eye, diag_blocks, level_masks)

    state = [s_scr[u] for u in range(N_UNITS)]
    ps = [_dot(jnp.concatenate([k_u * jnp.exp(g_u), q_u * jnp.exp(g_u)], axis=0), s_u)
          for k_u, q_u, g_u, s_u in zip(k, q, g_col, state)]
    v_new = [_dot(t_u, beta_u * (v_u - ps_u[:CHUNK]))
             for t_u, beta_u, v_u, ps_u in zip(t_mat, beta, v, ps)]
    for unit, ps_u, kq_u, decay_u, v_new_u in zip(units, ps, kq, decay, v_new):
        cols, o_ref = unit[3], unit[9]
        o_ref[0, :, cols] = ps_u[CHUNK:] + _dot(kq_u[CHUNK:] * decay_u, v_new_u)
    for u, (k_u, g_u, gl_u, s_u, v_new_u) in enumerate(zip(k, g_col, g_last, state, v_new)):
        k_tail = k_u * jnp.exp(gl_u - g_u)
        s_scr[u] = s_u * jnp.exp(gl_u) + _dot(k_tail.T, v_new_u)

    @pl.when(i == n_blocks - 1)
    def _():
        s_out[0] = s_scr[...]


def _gdn_scan(qn, kn, vv, gb, gr, state0):
    b, l, _ = qn.shape
    n = l // CHUNK
    fwd = lambda width: pl.BlockSpec((1, CHUNK, width), lambda bi, i: (bi, i, 0))
    bwd = lambda width: pl.BlockSpec((1, CHUNK, width), lambda bi, i: (bi, n - 1 - i, 0))
    gr_f = pl.BlockSpec((1, 2 * N_UNITS, CHUNK), lambda bi, i: (bi, 0, i))
    gr_b = pl.BlockSpec((1, 2 * N_UNITS, CHUNK), lambda bi, i: (bi, 0, n - 1 - i))
    st = pl.BlockSpec((1, N_UNITS, GDN_DK, GDN_DK), lambda bi, i: (bi, 0, 0, 0))
    return pl.pallas_call(
        functools.partial(_gdn_scan_kernel, n_blocks=n),
        grid=(b, n),
        in_specs=[fwd(GDN_QK_DIM), fwd(GDN_QK_DIM), fwd(GDN_V_DIM), fwd(LANES), gr_f,
                  bwd(GDN_QK_DIM), bwd(GDN_QK_DIM), bwd(GDN_V_DIM), bwd(LANES), gr_b, st],
        out_specs=[fwd(GDN_V_DIM), bwd(GDN_V_DIM), st],
        out_shape=[jax.ShapeDtypeStruct((b, l, GDN_V_DIM), F32),
                   jax.ShapeDtypeStruct((b, l, GDN_V_DIM), F32),
                   jax.ShapeDtypeStruct((b, N_UNITS, GDN_DK, GDN_DK), F32)],
        scratch_shapes=[pltpu.VMEM((N_UNITS, GDN_DK, GDN_DK), F32)],
        compiler_params=_params("parallel", "arbitrary"),
        name="gdn_scan",
    )(qn, kn, vv, gb, gr, qn, kn, vv, gb, gr, state0)


def _attn_kernel(q_ref, kvp_ref, kvc_ref, kvn_ref, ctx_ref, sink_ref, o_ref, *, n_blocks, n_ctx_tiles):
    i = pl.program_id(1)
    row = lax.broadcasted_iota(jnp.int32, (WINDOW, WINDOW), 0)
    col = lax.broadcasted_iota(jnp.int32, (WINDOW, WINDOW), 1)
    mask_prev = (col >= row) & (i > 0)
    mask_next = (col <= row) & (i < n_blocks - 1)
    def tiles(head, base):
        kv_head = head // (ATTN_Q_HEADS // ATTN_KV_HEADS)
        c = (base + 2 * kv_head + head % 2) * LANES
        out = [kvp_ref[0, :, c:c + LANES], kvc_ref[0, :, c:c + LANES], kvn_ref[0, :, c:c + LANES]]
        return out + [ctx_ref[0, t * WINDOW:(t + 1) * WINDOW, c:c + LANES] for t in range(n_ctx_tiles)]

    pairs = range(ATTN_Q_HEADS // 2)
    left = lax.broadcasted_iota(jnp.int32, (WINDOW, LANES), 1) < HEAD_DIM
    scores = []
    for pair in pairs:
        q = q_ref[0, :, pair * LANES:(pair + 1) * LANES]
        both = [lax.dot_general(q, jnp.concatenate([ka, kb], axis=0), _NT, preferred_element_type=F32)
                for ka, kb in zip(tiles(2 * pair, 0), tiles(2 * pair + 1, 0))]
        scores.append([t[:, :WINDOW] for t in both])
        scores.append([t[:, WINDOW:] for t in both])
    probs, inv_denoms = [], []
    for head, s in enumerate(scores):
        sink = sink_ref[head]
        s[0] = jnp.where(mask_prev, s[0], -jnp.inf)
        s[2] = jnp.where(mask_next, s[2], -jnp.inf)
        m = s[1]
        for st in s[:1] + s[2:]:
            m = jnp.maximum(m, st)
        m = jnp.maximum(jnp.max(m, axis=-1, keepdims=True), sink)
        e = [jnp.exp(st - m) for st in s]
        tot = e[0]
        for et in e[1:]:
            tot = tot + et
        inv_denoms.append(1.0 / (jnp.sum(tot, axis=-1, keepdims=True) + jnp.exp(sink - m)))
        probs.append([et.astype(BF16) for et in e])
    for pair in pairs:
        pv = None
        for ea, eb, va, vb in zip(probs[2 * pair], probs[2 * pair + 1],
                                  tiles(2 * pair, 4), tiles(2 * pair + 1, 4)):
            part = jnp.dot(jnp.concatenate([ea, eb], axis=1), jnp.concatenate([va, vb], axis=0),
                           preferred_element_type=F32)
            pv = part if pv is None else pv + part
        acc = pv * jnp.where(left, inv_denoms[2 * pair], inv_denoms[2 * pair + 1])
        o_ref[0, :, pair * LANES:(pair + 1) * LANES] = acc.astype(BF16)


def _attn(q, kvm, kvm_ctx, sink):
    b, l, _ = q.shape
    lc = kvm_ctx.shape[1]
    n = l // WINDOW
    blk = lambda f: pl.BlockSpec((1, WINDOW, KVM_WIDTH), f)
    return pl.pallas_call(
        functools.partial(_attn_kernel, n_blocks=n, n_ctx_tiles=lc // WINDOW),
        grid=(b, n),
        in_specs=[pl.BlockSpec((1, WINDOW, ATTN_Q_DIM), lambda bi, i: (bi, i, 0)),
                  blk(lambda bi, i: (bi, jnp.maximum(i - 1, 0), 0)),
                  blk(lambda bi, i: (bi, i, 0)),
                  blk(lambda bi, i: (bi, jnp.minimum(i + 1, n - 1), 0)),
                  pl.BlockSpec((1, lc, KVM_WIDTH), lambda bi, i: (bi, 0, 0)),
                  pl.BlockSpec(memory_space=pltpu.SMEM)],
        out_specs=pl.BlockSpec((1, WINDOW, ATTN_Q_DIM), lambda bi, i: (bi, i, 0)),
        out_shape=jax.ShapeDtypeStruct((b, l, ATTN_Q_DIM), BF16),
        compiler_params=_params("parallel", "arbitrary"),
        name="attn",
    )(q, kvm, kvm, kvm, kvm_ctx, sink)


def _residual_ffn(x, y, mod, gains, wg_ref, wu_ref, wd_ref):
    x1 = x + mod[2:3] * (_rms(y) * gains[0:1])
    h = _modulated_norm(x1, gains[1:2], mod, 3).astype(BF16)
    gate = jnp.dot(h, wg_ref[...], preferred_element_type=F32)
    up = jnp.dot(h, wu_ref[...], preferred_element_type=F32)
    f = _dot(_silu(gate) * up, wd_ref[...])
    return x1 + mod[5:6] * (_rms(f) * gains[2:3])


def _post_hy_kernel(x_ref, attn_ref, of_ref, ob_ref, z_ref, mod_ref, gains_ref, ng_ref,
                    wo_ref, wg_ref, wu_ref, wd_ref, out_ref):
    o = of_ref[0] + ob_ref[0]
    z = z_ref[0]
    y = jnp.dot(attn_ref[0], wo_ref[0:ATTN_Q_DIM, :], preferred_element_type=F32)
    for h in range(GDN_HEADS):
        cols = slice(h * GDN_DK, (h + 1) * GDN_DK)
        gated = _rms(o[:, cols]) * ng_ref[...] * _silu(z[:, cols])
        y = y + _dot(gated, wo_ref[ATTN_Q_DIM + h * GDN_DK:ATTN_Q_DIM + (h + 1) * GDN_DK, :])
    out_ref[0] = _residual_ffn(x_ref[0], y, mod_ref[0], gains_ref[...], wg_ref, wu_ref, wd_ref)


def _post_hy(x, attn, o_f, o_b, z, mod, gains, norm_g, w_out, w_gate, w_up, w_down, tm):
    b, l, d = x.shape
    hidden = w_gate.shape[1]
    seq = lambda width: pl.BlockSpec((1, tm, width), lambda bi, i: (bi, i, 0))
    return pl.pallas_call(
        _post_hy_kernel,
        grid=(b, l // tm),
        in_specs=[seq(d), seq(ATTN_Q_DIM), seq(GDN_V_DIM), seq(GDN_V_DIM), seq(GDN_V_DIM),
                  pl.BlockSpec((1, 6, d), lambda bi, i: (bi, 0, 0)),
                  _resident((3, d)), _resident((1, GDN_DK)),
                  _resident((ATTN_Q_DIM + GDN_V_DIM, d)),
                  _resident((d, hidden)), _resident((d, hidden)), _resident((hidden, d))],
        out_specs=seq(d),
        out_shape=jax.ShapeDtypeStruct((b, l, d), F32),
        compiler_params=_params("parallel", "arbitrary"),
        name="post_hy",
    )(x, attn, o_f, o_b, z, mod, gains, norm_g.reshape(1, GDN_DK), w_out, w_gate, w_up, w_down)


def _inproj_sc_kernel(x_ref, mod_ref, g_ref, w_ref, b_out, cu_out):
    d = x_ref.shape[2]
    h = _modulated_norm(x_ref[0], g_ref[...], mod_ref[0], 0)
    p = _dot(h, w_ref[...])
    b_out[0] = p[:, 0:d]
    cu_out[0] = p[:, d:2 * d] * p[:, 2 * d:3 * d]


def _inproj_sc(x, mod, gain, w_in, tm):
    b, l, d = x.shape
    seq = pl.BlockSpec((1, tm, d), lambda bi, i: (bi, i, 0))
    return pl.pallas_call(
        _inproj_sc_kernel,
        grid=(b, l // tm),
        in_specs=[seq, pl.BlockSpec((1, 6, d), lambda bi, i: (bi, 0, 0)),
                  _resident((1, d)), _resident((d, 3 * d))],
        out_specs=[seq, seq],
        out_shape=[jax.ShapeDtypeStruct((b, l, d), F32), jax.ShapeDtypeStruct((b, l, d), F32)],
        compiler_params=_params("parallel", "arbitrary"),
        name="inproj_sc",
    )(x, mod, gain.reshape(1, d), w_in)


def _post_sc_kernel(x_ref, b_ref, cu_ref, prev_ref, next_ref, mod_ref, gains_ref, cw_ref,
                    wo_ref, wg_ref, wu_ref, wd_ref, out_ref, *, n_tiles):
    i = pl.program_id(1)
    tm = x_ref.shape[1]
    row = lax.broadcasted_iota(jnp.int32, (tm, 1), 0)
    prev_row = jnp.where(i > 0, prev_ref[0, SUBLANES - 1:SUBLANES, :], 0.0)
    next_row = jnp.where(i < n_tiles - 1, next_ref[0, 0:1, :], 0.0)
    conv = _conv3_rows(cu_ref[0], prev_row, next_row, cw_ref[...], row)
    y = _dot(b_ref[0] * conv, wo_ref[...])
    out_ref[0] = _residual_ffn(x_ref[0], y, mod_ref[0], gains_ref[...], wg_ref, wu_ref, wd_ref)


def _post_sc(x, b_gate, cu, mod, gains, conv_w, w_out, w_gate, w_up, w_down, tm):
    b, l, d = x.shape
    hidden = w_gate.shape[1]
    n = l // tm
    seq = pl.BlockSpec((1, tm, d), lambda bi, i: (bi, i, 0))
    prev, nxt = _halo_specs(tm, d, l)
    return pl.pallas_call(
        functools.partial(_post_sc_kernel, n_tiles=n),
        grid=(b, n),
        in_specs=[seq, seq, seq, prev, nxt,
                  pl.BlockSpec((1, 6, d), lambda bi, i: (bi, 0, 0)),
                  _resident((3, d)), _resident((3, d)), _resident((d, d)),
                  _resident((d, hidden)), _resident((d, hidden)), _resident((hidden, d))],
        out_specs=seq,
        out_shape=jax.ShapeDtypeStruct((b, l, d), F32),
        compiler_params=_params("parallel", "arbitrary"),
        name="post_sc",
    )(x, b_gate, cu, cu, cu, mod, gains, conv_w, w_out, w_gate, w_up, w_down)


def _rope_tables(length):
    pos = jnp.arange(length)
    n_freq = HEAD_DIM // 4
    inv_freq = ROPE_BASE ** (-jnp.arange(n_freq, dtype=F32) / n_freq)
    ang = jnp.concatenate([(pos // GRID_W).astype(F32)[:, None] * inv_freq,
                           (pos % GRID_W).astype(F32)[:, None] * inv_freq], axis=-1)
    cos, sin = jnp.cos(ang), jnp.sin(ang)
    return jnp.tile(cos, (1, 4)), jnp.tile(jnp.concatenate([-sin, sin], axis=-1), (1, 2))


def _row_tile(length, target):
    tm = min(length, target)
    assert length % tm == 0 and tm % CHUNK == 0
    return tm


def kernel(x, c, ctx, c_ctx, ada_w, ada_b, pre_mix_g, post_mix_g, pre_ffn_g, post_ffn_g, hy_w_in, hy_w_out,
           attn_sink, gdn_conv_w, gdn_a_log, gdn_dt_bias, gdn_norm_g, sc_w_in, sc_conv_w, sc_w_out,
           ffn_w_gate, ffn_w_up, ffn_w_down):
    b, l, d = x.shape
    lc = ctx.shape[1]
    assert ada_w.shape[0] == 2 and l % CHUNK == 0 and lc % CHUNK == 0 and b + 1 <= SUBLANES
    tm = _row_tile(l, 512)
    tmc = _row_tile(lc, 256)

    cond = jnp.concatenate([c, c_ctx[None], jnp.zeros((SUBLANES - b - 1, d), F32)], axis=0)
    mod = _ada(cond, ada_w, ada_b).reshape(2, SUBLANES, 6, d)
    gains = jnp.stack([post_mix_g, pre_ffn_g, post_ffn_g], axis=1)
    bf = lambda w: w.astype(BF16)

    w_in = bf(jnp.pad(hy_w_in[0], ((0, 0), (0, HY_IN_PAD - hy_w_in.shape[2]))))
    cos, sin = _rope_tables(l)
    q, kvm, qkv, z, ab = _inproj_hy(x, mod[0], None, pre_mix_g[0], w_in, cos, sin, tm)
    ones, zeros = jnp.ones((lc, LANES), F32), jnp.zeros((lc, LANES), F32)
    _, kvm_c, qkv_c, _, ab_c = _inproj_hy(ctx, mod[0], b, pre_mix_g[0], w_in, ones, zeros, tmc)

    lat = _gdn_prep(qkv, ab, gdn_conv_w[0], gdn_a_log[0], gdn_dt_bias[0], tm)
    con = _gdn_prep(qkv_c, ab_c, gdn_conv_w[0], gdn_a_log[0], gdn_dt_bias[0], tmc)
    state0 = jnp.zeros((b, N_UNITS, GDN_DK, GDN_DK), F32)
    _, _, state_c = _gdn_scan(*con, state0)
    o_f, o_b, _ = _gdn_scan(*lat, state_c)

    attn = _attn(q, kvm, kvm_c, attn_sink[0])
    x = _post_hy(x, attn, o_f, o_b, z, mod[0], gains[0], gdn_norm_g[0], bf(hy_w_out[0]),
                 bf(ffn_w_gate[0]), bf(ffn_w_up[0]), bf(ffn_w_down[0]), tm)

    b_gate, cu = _inproj_sc(x, mod[1], pre_mix_g[1], bf(sc_w_in[0]), tm)
    return _post_sc(x, b_gate, cu, mod[1], gains[1], sc_conv_w[0], bf(sc_w_out[0]),
                    bf(ffn_w_gate[1]), bf(ffn_w_up[1]), bf(ffn_w_down[1]), tm)
```

```python
import functools

import jax
import jax.numpy as jnp
from jax import lax
from jax.experimental import pallas as pl
from jax.experimental.pallas import tpu as pltpu

F32 = jnp.float32
BF16 = jnp.bfloat16

EPS = 1e-6
GRID_W = 64
ROPE_BASE = 10000.0
HEAD_DIM = 64
ATTN_Q_HEADS = 8
ATTN_KV_HEADS = 2
ATTN_Q_DIM = ATTN_Q_HEADS * HEAD_DIM
ATTN_KV_DIM = ATTN_KV_HEADS * HEAD_DIM
WINDOW = 128
GDN_HEADS = 4
GDN_DK = 128
GDN_QK_DIM = GDN_HEADS * GDN_DK
GDN_V_DIM = GDN_HEADS * GDN_DK
GDN_QKV_DIM = 2 * GDN_QK_DIM + GDN_V_DIM
N_UNITS = 2 * GDN_HEADS
LANES = 128
SUBLANES = 8
CHUNK = 128
SCAN_ROWS = 4 * CHUNK
COL_Q = 0
COL_K = COL_Q + ATTN_Q_DIM
COL_V = COL_K + ATTN_KV_DIM
COL_QKV = COL_V + ATTN_KV_DIM
COL_Z = COL_QKV + GDN_QKV_DIM
COL_AB = COL_Z + GDN_V_DIM
HY_IN_PAD = COL_AB + LANES
KVM_WIDTH = 8 * LANES
VMEM_LIMIT = 56 * 1024 * 1024

_NT = (((1,), (1,)), ((), ()))


def _sigmoid(x):
    return 1.0 / (1.0 + jnp.exp(-x))


def _silu(x):
    return x * _sigmoid(x)


def _rms(x):
    return x * lax.rsqrt(jnp.mean(x * x, axis=-1, keepdims=True) + EPS)


def _dot(a, b):
    return jnp.dot(a.astype(BF16), b.astype(BF16), preferred_element_type=F32)


def _params(*sem):
    return pltpu.CompilerParams(dimension_semantics=sem, vmem_limit_bytes=VMEM_LIMIT)


def _resident(shape):
    zeros = (0,) * len(shape)
    return pl.BlockSpec(shape, lambda *_: zeros, pipeline_mode=pl.Buffered(1))


def _ada_kernel(cond_ref, w_ref, b_ref, out_ref):
    s = _silu(cond_ref[...])
    out_ref[0] = jnp.dot(s, w_ref[0], precision=lax.Precision.HIGHEST,
                         preferred_element_type=F32) + b_ref[0]


def _ada(cond, ada_w, ada_b, tn=1536):
    depth, d, n = ada_w.shape
    rows = cond.shape[0]
    return pl.pallas_call(
        _ada_kernel,
        grid=(depth, n // tn),
        in_specs=[pl.BlockSpec((rows, d), lambda l, j: (0, 0)),
                  pl.BlockSpec((1, d, tn), lambda l, j: (l, 0, j)),
                  pl.BlockSpec((1, 1, tn), lambda l, j: (l, 0, j))],
        out_specs=pl.BlockSpec((1, rows, tn), lambda l, j: (l, 0, j)),
        out_shape=jax.ShapeDtypeStruct((depth, rows, n), F32),
        compiler_params=_params("arbitrary", "arbitrary"),
        name="ada",
    )(cond, ada_w, ada_b.reshape(depth, 1, n))


def _modulated_norm(x, gain, mod, shift_row):
    return _rms(x) * gain * (1.0 + mod[shift_row + 1:shift_row + 2]) + mod[shift_row:shift_row + 1]


def _inproj_hy_kernel(x_ref, mod_ref, g_ref, w_ref, cos_ref, sin_ref,
                      q_out, kvm_out, qkv_out, z_out, ab_out):
    h = _modulated_norm(x_ref[0], g_ref[...], mod_ref[0], 0)
    p = _dot(h, w_ref[...])
    tm = p.shape[0]
    cos = cos_ref[...]
    sin = sin_ref[...]
    lane = lax.broadcasted_iota(jnp.int32, (tm, LANES), 1)
    first_half = (lane & (HEAD_DIM - 1)) < HEAD_DIM // 2
    left = lane < HEAD_DIM

    def rope(t):
        partner = jnp.where(first_half, pltpu.roll(t, LANES - HEAD_DIM // 2, 1),
                            pltpu.roll(t, HEAD_DIM // 2, 1))
        return t * cos + partner * sin

    for j in range(ATTN_Q_DIM // LANES):
        c0 = COL_Q + j * LANES
        q_out[0, :, j * LANES:(j + 1) * LANES] = (rope(p[:, c0:c0 + LANES]) * HEAD_DIM ** -0.5).astype(BF16)

    def masked_layouts(t):
        sw = pltpu.roll(t, HEAD_DIM, 1)
        return (jnp.where(left, t, 0.0), jnp.where(left, 0.0, sw),
                jnp.where(left, sw, 0.0), jnp.where(left, 0.0, t))

    k_lay = masked_layouts(rope(p[:, COL_K:COL_K + LANES]))
    v_lay = masked_layouts(p[:, COL_V:COL_V + LANES])
    for j, t in enumerate(k_lay + v_lay):
        kvm_out[0, :, j * LANES:(j + 1) * LANES] = t.astype(BF16)
    qkv_out[0] = p[:, COL_QKV:COL_Z]
    z_out[0] = p[:, COL_Z:COL_AB]
    ab_out[0] = p[:, COL_AB:HY_IN_PAD]


def _inproj_hy(x, mod, mod_row, gain, w_pad, cos, sin, tm):
    b, l, d = x.shape
    n = l // tm
    row_of = (lambda bi: bi) if mod_row is None else (lambda bi: mod_row)
    seq = lambda width: pl.BlockSpec((1, tm, width), lambda bi, i: (bi, i, 0))
    return pl.pallas_call(
        _inproj_hy_kernel,
        grid=(b, n),
        in_specs=[seq(d),
                  pl.BlockSpec((1, 6, d), lambda bi, i: (row_of(bi), 0, 0)),
                  _resident((1, d)),
                  _resident((d, HY_IN_PAD)),
                  pl.BlockSpec((tm, LANES), lambda bi, i: (i, 0)),
                  pl.BlockSpec((tm, LANES), lambda bi, i: (i, 0))],
        out_specs=[seq(ATTN_Q_DIM), seq(KVM_WIDTH), seq(GDN_QKV_DIM), seq(GDN_V_DIM), seq(LANES)],
        out_shape=[jax.ShapeDtypeStruct((b, l, ATTN_Q_DIM), BF16),
                   jax.ShapeDtypeStruct((b, l, KVM_WIDTH), BF16),
                   jax.ShapeDtypeStruct((b, l, GDN_QKV_DIM), F32),
                   jax.ShapeDtypeStruct((b, l, GDN_V_DIM), F32),
                   jax.ShapeDtypeStruct((b, l, LANES), F32)],
        compiler_params=_params("parallel", "arbitrary"),
        name="inproj_hy",
    )(x, mod, gain.reshape(1, d), w_pad, cos, sin)


def _conv3_rows(x, prev_row, next_row, w, row):
    tm = x.shape[0]
    xp = jnp.where(row == 0, prev_row, pltpu.roll(x, 1, 0))
    xn = jnp.where(row == tm - 1, next_row, pltpu.roll(x, tm - 1, 0))
    return w[0:1] * xp + w[1:2] * x + w[2:3] * xn


def _halo_specs(tm, width, n_rows):
    per = tm // SUBLANES
    last = n_rows // SUBLANES - 1
    prev = pl.BlockSpec((1, SUBLANES, width), lambda bi, i: (bi, jnp.maximum(i * per - 1, 0), 0))
    nxt = pl.BlockSpec((1, SUBLANES, width), lambda bi, i: (bi, jnp.minimum((i + 1) * per, last), 0))
    return prev, nxt


def _gdn_prep_kernel(x_ref, prev_ref, next_ref, ab_ref, cw_ref, par_ref,
                     q_out, k_out, v_out, gb_out, gr_out, *, n_tiles):
    i = pl.program_id(1)
    tm = x_ref.shape[1]
    row = lax.broadcasted_iota(jnp.int32, (tm, 1), 0)
    has_prev = i > 0
    has_next = i < n_tiles - 1
    for j in range(GDN_QKV_DIM // LANES):
        cols = slice(j * LANES, (j + 1) * LANES)
        prev_row = jnp.where(has_prev, prev_ref[0, SUBLANES - 1:SUBLANES, cols], 0.0)
        next_row = jnp.where(has_next, next_ref[0, 0:1, cols], 0.0)
        y = _silu(_conv3_rows(x_ref[0, :, cols], prev_row, next_row, cw_ref[:, cols], row))
        if j < GDN_HEADS:
            y = y * lax.rsqrt(jnp.sum(y * y, axis=-1, keepdims=True) + EPS) * GDN_DK ** -0.5
            q_out[0, :, cols] = y
        elif j < 2 * GDN_HEADS:
            y = y * lax.rsqrt(jnp.sum(y * y, axis=-1, keepdims=True) + EPS)
            k_out[0, :, j * LANES - GDN_QK_DIM:(j + 1) * LANES - GDN_QK_DIM] = y
        else:
            v_out[0, :, j * LANES - 2 * GDN_QK_DIM:(j + 1) * LANES - 2 * GDN_QK_DIM] = y

    ab = ab_ref[0]
    lane = lax.broadcasted_iota(jnp.int32, (tm, LANES), 1)
    t = ab + par_ref[1:2]
    softplus = jnp.maximum(t, 0.0) + jnp.log1p(jnp.exp(-jnp.abs(t)))
    g = -jnp.exp(par_ref[0:1]) * softplus
    beta = _sigmoid(ab)
    pos = row & (CHUNK - 1)
    pre = g
    suf = g
    s = 1
    while s < CHUNK:
        pre = pre + jnp.where(pos >= s, pltpu.roll(pre, s, 0), 0.0)
        suf = suf + jnp.where(pos < CHUNK - s, pltpu.roll(suf, tm - s, 0), 0.0)
        s *= 2
    gb = jnp.where(lane < GDN_HEADS, pre, jnp.where(lane < N_UNITS, suf, beta))
    gb_out[0] = gb
    gr_out[0] = gb.T[0:2 * N_UNITS]


def _gdn_prep(qkv, ab, conv_w, a_log, dt_bias, tm):
    b, l, _ = qkv.shape
    n = l // tm
    par = jnp.zeros((SUBLANES, LANES), F32)
    par = par.at[0, :N_UNITS].set(a_log.reshape(-1)).at[1, :N_UNITS].set(dt_bias.reshape(-1))
    prev, nxt = _halo_specs(tm, GDN_QKV_DIM, l)
    seq = lambda width: pl.BlockSpec((1, tm, width), lambda bi, i: (bi, i, 0))
    return pl.pallas_call(
        functools.partial(_gdn_prep_kernel, n_tiles=n),
        grid=(b, n),
        in_specs=[seq(GDN_QKV_DIM), prev, nxt, seq(LANES),
                  _resident((3, GDN_QKV_DIM)), _resident((SUBLANES, LANES))],
        out_specs=[seq(GDN_QK_DIM), seq(GDN_QK_DIM), seq(GDN_V_DIM), seq(LANES),
                   pl.BlockSpec((1, 2 * N_UNITS, tm), lambda bi, i: (bi, 0, i))],
        out_shape=[jax.ShapeDtypeStruct((b, l, GDN_QK_DIM), F32),
                   jax.ShapeDtypeStruct((b, l, GDN_QK_DIM), F32),
                   jax.ShapeDtypeStruct((b, l, GDN_V_DIM), F32),
                   jax.ShapeDtypeStruct((b, l, LANES), F32),
                   jax.ShapeDtypeStruct((b, 2 * N_UNITS, l), F32)],
        compiler_params=_params("parallel", "arbitrary"),
        name="gdn_prep",
    )(qkv, qkv, qkv, ab, conv_w, par)


def _unit_triangular_inverses(a_list, eye, diag_blocks, level_masks):
    ad = [jnp.where(diag_blocks, a, 0.0) for a in a_list]
    a2 = [_dot(x, x) for x in ad]
    a4 = [_dot(x, x) for x in a2]
    t = [_dot(eye - x, eye + y) for x, y in zip(ad, a2)]
    a8 = [_dot(x, x) for x in a4]
    t = [_dot(x, eye + y) for x, y in zip(t, a4)]
    t = [_dot(x, eye + y) for x, y in zip(t, a8)]
    for m in level_masks:
        at = [_dot(jnp.where(m, a, 0.0), x) for a, x in zip(a_list, t)]
        t = [x - _dot(x, y) for x, y in zip(t, at)]
    return t


def _gdn_scan_kernel(qf_ref, kf_ref, vf_ref, gbf_ref, grf_ref,
                     qb_ref, kb_ref, vb_ref, gbb_ref, grb_ref, s0_ref,
                     of_ref, ob_ref, s_out, s_scr, *, n_blocks):
    i = pl.program_id(1)

    @pl.when(i == 0)
    def _():
        s_scr[...] = s0_ref[0]

    row = lax.broadcasted_iota(jnp.int32, (CHUNK, CHUNK), 0)
    col = lax.broadcasted_iota(jnp.int32, (CHUNK, CHUNK), 1)
    eye = jnp.where(row == col, 1.0, 0.0)
    same = lambda shift: (row >> shift) == (col >> shift)
    diag_blocks = same(4)
    level_masks = [same(sh) & jnp.logical_not(same(sh - 1)) for sh in (5, 6, 7)]

    units = []
    for d, refs in enumerate(((qf_ref, kf_ref, vf_ref, gbf_ref, grf_ref, of_ref),
                              (qb_ref, kb_ref, vb_ref, gbb_ref, grb_ref, ob_ref))):
        for h in range(GDN_HEADS):
            units.append((d, h, d * GDN_HEADS + h, slice(h * GDN_DK, (h + 1) * GDN_DK)) + refs)

    n_chunks = qf_ref.shape[1] // CHUNK
    q, k, v, g_col, beta, g_last, decay, kq, a = ({} for _ in range(9))
    for c in range(n_chunks):
        rows = slice(c * CHUNK, (c + 1) * CHUNK)
        for d, h, u, cols, q_ref, k_ref, v_ref, gb_ref, gr_ref, o_ref in units:
            incl = (row >= col) if d == 0 else (row <= col)
            last = CHUNK - 1 if d == 0 else 0
            key = (c, u)
            q[key] = q_ref[0, rows, cols]
            k[key] = k_ref[0, rows, cols]
            v[key] = v_ref[0, rows, cols]
            g_col[key] = gb_ref[0, rows, u:u + 1]
            beta[key] = gb_ref[0, rows, N_UNITS + u:N_UNITS + u + 1]
            g_last[key] = g_col[key][last:last + 1, :]
            decay[key] = jnp.exp(jnp.where(incl, g_col[key] - gr_ref[0, u:u + 1, rows], -jnp.inf))
            kq[key] = lax.dot_general(jnp.concatenate([k[key], q[key]], axis=0).astype(BF16),
                                      k[key].astype(BF16), _NT, preferred_element_type=F32)
    keys = list(kq)
    for key in keys:
        strict = (row > col) if key[1] < GDN_HEADS else (row < col)
        a[key] = jnp.where(strict, beta[key] * kq[key][:CHUNK] * decay[key], 0.0)
    t_mat = dict(zip(keys, _unit_triangular_inverses([a[key] for key in keys], eye, diag_blocks, level_masks)))

    state = [s_scr[u] for u in range(N_UNITS)]
    for step in range(n_chunks):
        order = [(step if unit[0] == 0 else n_chunks - 1 - step, unit[2]) for unit in units]
        ps = [_dot(jnp.concatenate([k[key] * jnp.exp(g_col[key]), q[key] * jnp.exp(g_col[key])], axis=0),
                   state[key[1]]) for key in order]
        v_new = [_dot(t_mat[key], beta[key] * (v[key] - ps_u[:CHUNK])) for key, ps_u in zip(order, ps)]
        for unit, key, ps_u, v_new_u in zip(units, order, ps, v_new):
            cols, o_ref = unit[3], unit[9]
            o_ref[0, key[0] * CHUNK:(key[0] + 1) * CHUNK, cols] = (
                ps_u[CHUNK:] + _dot(kq[key][CHUNK:] * decay[key], v_new_u))
        for key, v_new_u in zip(order, v_new):
            k_tail = k[key] * jnp.exp(g_last[key] - g_col[key])
            state[key[1]] = state[key[1]] * jnp.exp(g_last[key]) + _dot(k_tail.T, v_new_u)
    for u in range(N_UNITS):
        s_scr[u] = state[u]

    @pl.when(i == n_blocks - 1)
    def _():
        s_out[0] = s_scr[...]


def _gdn_scan(qn, kn, vv, gb, gr, state0, rows=SCAN_ROWS):
    b, l, _ = qn.shape
    n = l // rows
    fwd = lambda width: pl.BlockSpec((1, rows, width), lambda bi, i: (bi, i, 0))
    bwd = lambda width: pl.BlockSpec((1, rows, width), lambda bi, i: (bi, n - 1 - i, 0))
    gr_f = pl.BlockSpec((1, 2 * N_UNITS, rows), lambda bi, i: (bi, 0, i))
    gr_b = pl.BlockSpec((1, 2 * N_UNITS, rows), lambda bi, i: (bi, 0, n - 1 - i))
    st = pl.BlockSpec((1, N_UNITS, GDN_DK, GDN_DK), lambda bi, i: (bi, 0, 0, 0))
    return pl.pallas_call(
        functools.partial(_gdn_scan_kernel, n_blocks=n),
        grid=(b, n),
        in_specs=[fwd(GDN_QK_DIM), fwd(GDN_QK_DIM), fwd(GDN_V_DIM), fwd(LANES), gr_f,
                  bwd(GDN_QK_DIM), bwd(GDN_QK_DIM), bwd(GDN_V_DIM), bwd(LANES), gr_b, st],
        out_specs=[fwd(GDN_V_DIM), bwd(GDN_V_DIM), st],
        out_shape=[jax.ShapeDtypeStruct((b, l, GDN_V_DIM), F32),
                   jax.ShapeDtypeStruct((b, l, GDN_V_DIM), F32),
                   jax.ShapeDtypeStruct((b, N_UNITS, GDN_DK, GDN_DK), F32)],
        scratch_shapes=[pltpu.VMEM((N_UNITS, GDN_DK, GDN_DK), F32)],
        compiler_params=_params("parallel", "arbitrary"),
        name="gdn_scan",
    )(qn, kn, vv, gb, gr, qn, kn, vv, gb, gr, state0)


def _attn_kernel(q_ref, kvp_ref, kvc_ref, kvn_ref, ctx_ref, sink_ref, o_ref, *, n_blocks, n_ctx_tiles):
    i = pl.program_id(1)
    row = lax.broadcasted_iota(jnp.int32, (WINDOW, WINDOW), 0)
    col = lax.broadcasted_iota(jnp.int32, (WINDOW, WINDOW), 1)
    mask_prev = (col >= row) & (i > 0)
    mask_next = (col <= row) & (i < n_blocks - 1)
    def tiles(head, base):
        kv_head = head // (ATTN_Q_HEADS // ATTN_KV_HEADS)
        c = (base + 2 * kv_head + head % 2) * LANES
        out = [kvp_ref[0, :, c:c + LANES], kvc_ref[0, :, c:c + LANES], kvn_ref[0, :, c:c + LANES]]
        return out + [ctx_ref[0, t * WINDOW:(t + 1) * WINDOW, c:c + LANES] for t in range(n_ctx_tiles)]

    pairs = range(ATTN_Q_HEADS // 2)
    left = lax.broadcasted_iota(jnp.int32, (WINDOW, LANES), 1) < HEAD_DIM
    scores = []
    for pair in pairs:
        q = q_ref[0, :, pair * LANES:(pair + 1) * LANES]
        both = [lax.dot_general(q, jnp.concatenate([ka, kb], axis=0), _NT, preferred_element_type=F32)
                for ka, kb in zip(tiles(2 * pair, 0), tiles(2 * pair + 1, 0))]
        scores.append([t[:, :WINDOW] for t in both])
        scores.append([t[:, WINDOW:] for t in both])
    probs, inv_denoms = [], []
    for head, s in enumerate(scores):
        sink = sink_ref[head]
        s[0] = jnp.where(mask_prev, s[0], -jnp.inf)
        s[2] = jnp.where(mask_next, s[2], -jnp.inf)
        m = s[1]
        for st in s[:1] + s[2:]:
            m = jnp.maximum(m, st)
        m = jnp.maximum(jnp.max(m, axis=-1, keepdims=True), sink)
        e = [jnp.exp(st - m) for st in s]
        tot = e[0]
        for et in e[1:]:
            tot = tot + et
        inv_denoms.append(1.0 / (jnp.sum(tot, axis=-1, keepdims=True) + jnp.exp(sink - m)))
        probs.append([et.astype(BF16) for et in e])
    for pair in pairs:
        pv = None
        for ea, eb, va, vb in zip(probs[2 * pair], probs[2 * pair + 1],
                                  tiles(2 * pair, 4), tiles(2 * pair + 1, 4)):
            part = jnp.dot(jnp.concatenate([ea, eb], axis=1), jnp.concatenate([va, vb], axis=0),
                           preferred_element_type=F32)
            pv = part if pv is None else pv + part
        acc = pv * jnp.where(left, inv_denoms[2 * pair], inv_denoms[2 * pair + 1])
        o_ref[0, :, pair * LANES:(pair + 1) * LANES] = acc.astype(BF16)


def _attn(q, kvm, kvm_ctx, sink):
    b, l, _ = q.shape
    lc = kvm_ctx.shape[1]
    n = l // WINDOW
    blk = lambda f: pl.BlockSpec((1, WINDOW, KVM_WIDTH), f)
    return pl.pallas_call(
        functools.partial(_attn_kernel, n_blocks=n, n_ctx_tiles=lc // WINDOW),
        grid=(b, n),
        in_specs=[pl.BlockSpec((1, WINDOW, ATTN_Q_DIM), lambda bi, i: (bi, i, 0)),
                  blk(lambda bi, i: (bi, jnp.maximum(i - 1, 0), 0)),
                  blk(lambda bi, i: (bi, i, 0)),
                  blk(lambda bi, i: (bi, jnp.minimum(i + 1, n - 1), 0)),
                  pl.BlockSpec((1, lc, KVM_WIDTH), lambda bi, i: (bi, 0, 0)),
                  pl.BlockSpec(memory_space=pltpu.SMEM)],
        out_specs=pl.BlockSpec((1, WINDOW, ATTN_Q_DIM), lambda bi, i: (bi, i, 0)),
        out_shape=jax.ShapeDtypeStruct((b, l, ATTN_Q_DIM), BF16),
        compiler_params=_params("parallel", "arbitrary"),
        name="attn",
    )(q, kvm, kvm, kvm, kvm_ctx, sink)


def _residual_ffn(x, y, mod, gains, wg_ref, wu_ref, wd_ref):
    x1 = x + mod[2:3] * (_rms(y) * gains[0:1])
    h = _modulated_norm(x1, gains[1:2], mod, 3).astype(BF16)
    gate = jnp.dot(h, wg_ref[...], preferred_element_type=F32)
    up = jnp.dot(h, wu_ref[...], preferred_element_type=F32)
    f = _dot(_silu(gate) * up, wd_ref[...])
    return x1 + mod[5:6] * (_rms(f) * gains[2:3])


def _post_hy_kernel(x_ref, attn_ref, of_ref, ob_ref, z_ref, mod_ref, gains_ref, ng_ref,
                    wo_ref, wg_ref, wu_ref, wd_ref, out_ref):
    o = of_ref[0] + ob_ref[0]
    z = z_ref[0]
    y = jnp.dot(attn_ref[0], wo_ref[0:ATTN_Q_DIM, :], preferred_element_type=F32)
    for h in range(GDN_HEADS):
        cols = slice(h * GDN_DK, (h + 1) * GDN_DK)
        gated = _rms(o[:, cols]) * ng_ref[...] * _silu(z[:, cols])
        y = y + _dot(gated, wo_ref[ATTN_Q_DIM + h * GDN_DK:ATTN_Q_DIM + (h + 1) * GDN_DK, :])
    out_ref[0] = _residual_ffn(x_ref[0], y, mod_ref[0], gains_ref[...], wg_ref, wu_ref, wd_ref)


def _post_hy(x, attn, o_f, o_b, z, mod, gains, norm_g, w_out, w_gate, w_up, w_down, tm):
    b, l, d = x.shape
    hidden = w_gate.shape[1]
    seq = lambda width: pl.BlockSpec((1, tm, width), lambda bi, i: (bi, i, 0))
    return pl.pallas_call(
        _post_hy_kernel,
        grid=(b, l // tm),
        in_specs=[seq(d), seq(ATTN_Q_DIM), seq(GDN_V_DIM), seq(GDN_V_DIM), seq(GDN_V_DIM),
                  pl.BlockSpec((1, 6, d), lambda bi, i: (bi, 0, 0)),
                  _resident((3, d)), _resident((1, GDN_DK)),
                  _resident((ATTN_Q_DIM + GDN_V_DIM, d)),
                  _resident((d, hidden)), _resident((d, hidden)), _resident((hidden, d))],
        out_specs=seq(d),
        out_shape=jax.ShapeDtypeStruct((b, l, d), F32),
        compiler_params=_params("parallel", "arbitrary"),
        name="post_hy",
    )(x, attn, o_f, o_b, z, mod, gains, norm_g.reshape(1, GDN_DK), w_out, w_gate, w_up, w_down)


def _inproj_sc_kernel(x_ref, mod_ref, g_ref, w_ref, b_out, cu_out):
    d = x_ref.shape[2]
    h = _modulated_norm(x_ref[0], g_ref[...], mod_ref[0], 0)
    p = _dot(h, w_ref[...])
    b_out[0] = p[:, 0:d]
    cu_out[0] = p[:, d:2 * d] * p[:, 2 * d:3 * d]


def _inproj_sc(x, mod, gain, w_in, tm):
    b, l, d = x.shape
    seq = pl.BlockSpec((1, tm, d), lambda bi, i: (bi, i, 0))
    return pl.pallas_call(
        _inproj_sc_kernel,
        grid=(b, l // tm),
        in_specs=[seq, pl.BlockSpec((1, 6, d), lambda bi, i: (bi, 0, 0)),
                  _resident((1, d)), _resident((d, 3 * d))],
        out_specs=[seq, seq],
        out_shape=[jax.ShapeDtypeStruct((b, l, d), F32), jax.ShapeDtypeStruct((b, l, d), F32)],
        compiler_params=_params("parallel", "arbitrary"),
        name="inproj_sc",
    )(x, mod, gain.reshape(1, d), w_in)


def _post_sc_kernel(x_ref, b_ref, cu_ref, prev_ref, next_ref, mod_ref, gains_ref, cw_ref,
                    wo_ref, wg_ref, wu_ref, wd_ref, out_ref, *, n_tiles):
    i = pl.program_id(1)
    tm = x_ref.shape[1]
    row = lax.broadcasted_iota(jnp.int32, (tm, 1), 0)
    prev_row = jnp.where(i > 0, prev_ref[0, SUBLANES - 1:SUBLANES, :], 0.0)
    next_row = jnp.where(i < n_tiles - 1, next_ref[0, 0:1, :], 0.0)
    conv = _conv3_rows(cu_ref[0], prev_row, next_row, cw_ref[...], row)
    y = _dot(b_ref[0] * conv, wo_ref[...])
    out_ref[0] = _residual_ffn(x_ref[0], y, mod_ref[0], gains_ref[...], wg_ref, wu_ref, wd_ref)


def _post_sc(x, b_gate, cu, mod, gains, conv_w, w_out, w_gate, w_up, w_down, tm):
    b, l, d = x.shape
    hidden = w_gate.shape[1]
    n = l // tm
    seq = pl.BlockSpec((1, tm, d), lambda bi, i: (bi, i, 0))
    prev, nxt = _halo_specs(tm, d, l)
    return pl.pallas_call(
        functools.partial(_post_sc_kernel, n_tiles=n),
        grid=(b, n),
        in_specs=[seq, seq, seq, prev, nxt,
                  pl.BlockSpec((1, 6, d), lambda bi, i: (bi, 0, 0)),
                  _resident((3, d)), _resident((3, d)), _resident((d, d)),
                  _resident((d, hidden)), _resident((d, hidden)), _resident((hidden, d))],
        out_specs=seq,
        out_shape=jax.ShapeDtypeStruct((b, l, d), F32),
        compiler_params=_params("parallel", "arbitrary"),
        name="post_sc",
    )(x, b_gate, cu, cu, cu, mod, gains, conv_w, w_out, w_gate, w_up, w_down)


def _rope_tables(length):
    pos = jnp.arange(length)
    n_freq = HEAD_DIM // 4
    inv_freq = ROPE_BASE ** (-jnp.arange(n_freq, dtype=F32) / n_freq)
    ang = jnp.concatenate([(pos // GRID_W).astype(F32)[:, None] * inv_freq,
                           (pos % GRID_W).astype(F32)[:, None] * inv_freq], axis=-1)
    cos, sin = jnp.cos(ang), jnp.sin(ang)
    return jnp.tile(cos, (1, 4)), jnp.tile(jnp.concatenate([-sin, sin], axis=-1), (1, 2))


def _row_tile(length, target):
    tm = min(length, target)
    assert length % tm == 0 and tm % CHUNK == 0
    return tm


def kernel(x, c, ctx, c_ctx, ada_w, ada_b, pre_mix_g, post_mix_g, pre_ffn_g, post_ffn_g, hy_w_in, hy_w_out,
           attn_sink, gdn_conv_w, gdn_a_log, gdn_dt_bias, gdn_norm_g, sc_w_in, sc_conv_w, sc_w_out,
           ffn_w_gate, ffn_w_up, ffn_w_down):
    b, l, d = x.shape
    lc = ctx.shape[1]
    assert ada_w.shape[0] == 2 and l % CHUNK == 0 and lc % CHUNK == 0 and b + 1 <= SUBLANES
    tm = _row_tile(l, 512)
    tmc = _row_tile(lc, 256)

    cond = jnp.concatenate([c, c_ctx[None], jnp.zeros((SUBLANES - b - 1, d), F32)], axis=0)
    mod = _ada(cond, ada_w, ada_b).reshape(2, SUBLANES, 6, d)
    gains = jnp.stack([post_mix_g, pre_ffn_g, post_ffn_g], axis=1)
    bf = lambda w: w.astype(BF16)

    w_in = bf(jnp.pad(hy_w_in[0], ((0, 0), (0, HY_IN_PAD - hy_w_in.shape[2]))))
    cos, sin = _rope_tables(l)
    q, kvm, qkv, z, ab = _inproj_hy(x, mod[0], None, pre_mix_g[0], w_in, cos, sin, tm)
    ones, zeros = jnp.ones((lc, LANES), F32), jnp.zeros((lc, LANES), F32)
    _, kvm_c, qkv_c, _, ab_c = _inproj_hy(ctx, mod[0], b, pre_mix_g[0], w_in, ones, zeros, tmc)

    lat = _gdn_prep(qkv, ab, gdn_conv_w[0], gdn_a_log[0], gdn_dt_bias[0], tm)
    con = _gdn_prep(qkv_c, ab_c, gdn_conv_w[0], gdn_a_log[0], gdn_dt_bias[0], tmc)
    state0 = jnp.zeros((b, N_UNITS, GDN_DK, GDN_DK), F32)
    _, _, state_c = _gdn_scan(*con, state0, _row_tile(lc, SCAN_ROWS))
    o_f, o_b, _ = _gdn_scan(*lat, state_c, _row_tile(l, SCAN_ROWS))

    attn = _attn(q, kvm, kvm_c, attn_sink[0])
    x = _post_hy(x, attn, o_f, o_b, z, mod[0], gains[0], gdn_norm_g[0], bf(hy_w_out[0]),
                 bf(ffn_w_gate[0]), bf(ffn_w_up[0]), bf(ffn_w_down[0]), tm)

    b_gate, cu = _inproj_sc(x, mod[1], pre_mix_g[1], bf(sc_w_in[0]), tm)
    return _post_sc(x, b_gate, cu, mod[1], gains[1], sc_conv_w[0], bf(sc_w_out[0]),
                    bf(ffn_w_gate[1]), bf(ffn_w_up[1]), bf(ffn_w_down[1]), tm)
```

```python
import functools

import jax
import jax.numpy as jnp
from jax import lax
from jax.experimental import pallas as pl
from jax.experimental.pallas import tpu as pltpu

F32 = jnp.float32
BF16 = jnp.bfloat16

EPS = 1e-6
GRID_W = 64
ROPE_BASE = 10000.0
HEAD_DIM = 64
ATTN_Q_HEADS = 8
ATTN_KV_HEADS = 2
ATTN_Q_DIM = ATTN_Q_HEADS * HEAD_DIM
ATTN_KV_DIM = ATTN_KV_HEADS * HEAD_DIM
WINDOW = 128
GDN_HEADS = 4
GDN_DK = 128
GDN_QK_DIM = GDN_HEADS * GDN_DK
GDN_V_DIM = GDN_HEADS * GDN_DK
GDN_QKV_DIM = 2 * GDN_QK_DIM + GDN_V_DIM
N_UNITS = 2 * GDN_HEADS
LANES = 128
SUBLANES = 8
CHUNK = 128
SCAN_ROWS = 4 * CHUNK
FFN_PARTS = 2
COL_Q = 0
COL_K = COL_Q + ATTN_Q_DIM
COL_V = COL_K + ATTN_KV_DIM
COL_QKV = COL_V + ATTN_KV_DIM
COL_Z = COL_QKV + GDN_QKV_DIM
COL_AB = COL_Z + GDN_V_DIM
HY_IN_PAD = COL_AB + LANES
KVM_WIDTH = 8 * LANES
VMEM_LIMIT = 56 * 1024 * 1024

_NT = (((1,), (1,)), ((), ()))


def _sigmoid(x):
    return 1.0 / (1.0 + jnp.exp(-x))


def _silu(x):
    return x * _sigmoid(x)


def _rms(x):
    return x * lax.rsqrt(jnp.mean(x * x, axis=-1, keepdims=True) + EPS)


def _dot(a, b):
    return jnp.dot(a.astype(BF16), b.astype(BF16), preferred_element_type=F32)


def _params(*sem):
    return pltpu.CompilerParams(dimension_semantics=sem, vmem_limit_bytes=VMEM_LIMIT)


def _resident(shape):
    zeros = (0,) * len(shape)
    return pl.BlockSpec(shape, lambda *_: zeros, pipeline_mode=pl.Buffered(1))


def _ada_kernel(cond_ref, w_ref, b_ref, out_ref):
    s = _silu(cond_ref[...])
    out_ref[0] = jnp.dot(s, w_ref[0], precision=lax.Precision.HIGHEST,
                         preferred_element_type=F32) + b_ref[0]


def _ada(cond, ada_w, ada_b, tn=1536):
    depth, d, n = ada_w.shape
    rows = cond.shape[0]
    return pl.pallas_call(
        _ada_kernel,
        grid=(depth, n // tn),
        in_specs=[pl.BlockSpec((rows, d), lambda l, j: (0, 0)),
                  pl.BlockSpec((1, d, tn), lambda l, j: (l, 0, j)),
                  pl.BlockSpec((1, 1, tn), lambda l, j: (l, 0, j))],
        out_specs=pl.BlockSpec((1, rows, tn), lambda l, j: (l, 0, j)),
        out_shape=jax.ShapeDtypeStruct((depth, rows, n), F32),
        compiler_params=_params("arbitrary", "arbitrary"),
        name="ada",
    )(cond, ada_w, ada_b.reshape(depth, 1, n))


def _modulated_norm(x, gain, mod, shift_row):
    return _rms(x) * gain * (1.0 + mod[shift_row + 1:shift_row + 2]) + mod[shift_row:shift_row + 1]


def _inproj_hy_kernel(x_ref, mod_ref, g_ref, w_ref, cos_ref, sin_ref,
                      q_out, kvm_out, qkv_out, z_out, ab_out):
    parts = _row_parts(x_ref.shape[1])
    hs = [_modulated_norm(x_ref[0, rows, :], g_ref[...], mod_ref[0], 0) for rows in parts]
    ps = [_dot(h, w_ref[...]) for h in hs]
    for rows, p in zip(parts, ps):
        cos = cos_ref[rows, :]
        sin = sin_ref[rows, :]
        lane = lax.broadcasted_iota(jnp.int32, (p.shape[0], LANES), 1)
        first_half = (lane & (HEAD_DIM - 1)) < HEAD_DIM // 2
        left = lane < HEAD_DIM

        def rope(t):
            partner = jnp.where(first_half, pltpu.roll(t, LANES - HEAD_DIM // 2, 1),
                                pltpu.roll(t, HEAD_DIM // 2, 1))
            return t * cos + partner * sin

        for j in range(ATTN_Q_DIM // LANES):
            c0 = COL_Q + j * LANES
            q_out[0, rows, j * LANES:(j + 1) * LANES] = (
                rope(p[:, c0:c0 + LANES]) * HEAD_DIM ** -0.5).astype(BF16)

        def masked_layouts(t):
            sw = pltpu.roll(t, HEAD_DIM, 1)
            return (jnp.where(left, t, 0.0), jnp.where(left, 0.0, sw),
                    jnp.where(left, sw, 0.0), jnp.where(left, 0.0, t))

        k_lay = masked_layouts(rope(p[:, COL_K:COL_K + LANES]))
        v_lay = masked_layouts(p[:, COL_V:COL_V + LANES])
        for j, t in enumerate(k_lay + v_lay):
            kvm_out[0, rows, j * LANES:(j + 1) * LANES] = t.astype(BF16)
        qkv_out[0, rows, :] = p[:, COL_QKV:COL_Z]
        z_out[0, rows, :] = p[:, COL_Z:COL_AB]
        ab_out[0, rows, :] = p[:, COL_AB:HY_IN_PAD]


def _inproj_hy(x, mod, mod_row, gain, w_pad, cos, sin, tm):
    b, l, d = x.shape
    n = l // tm
    row_of = (lambda bi: bi) if mod_row is None else (lambda bi: mod_row)
    seq = lambda width: pl.BlockSpec((1, tm, width), lambda bi, i: (bi, i, 0))
    return pl.pallas_call(
        _inproj_hy_kernel,
        grid=(b, n),
        in_specs=[seq(d),
                  pl.BlockSpec((1, 6, d), lambda bi, i: (row_of(bi), 0, 0)),
                  _resident((1, d)),
                  _resident((d, HY_IN_PAD)),
                  pl.BlockSpec((tm, LANES), lambda bi, i: (i, 0)),
                  pl.BlockSpec((tm, LANES), lambda bi, i: (i, 0))],
        out_specs=[seq(ATTN_Q_DIM), seq(KVM_WIDTH), seq(GDN_QKV_DIM), seq(GDN_V_DIM), seq(LANES)],
        out_shape=[jax.ShapeDtypeStruct((b, l, ATTN_Q_DIM), BF16),
                   jax.ShapeDtypeStruct((b, l, KVM_WIDTH), BF16),
                   jax.ShapeDtypeStruct((b, l, GDN_QKV_DIM), F32),
                   jax.ShapeDtypeStruct((b, l, GDN_V_DIM), F32),
                   jax.ShapeDtypeStruct((b, l, LANES), F32)],
        compiler_params=_params("parallel", "arbitrary"),
        name="inproj_hy",
    )(x, mod, gain.reshape(1, d), w_pad, cos, sin)


def _conv3_rows(x, prev_row, next_row, w, row):
    tm = x.shape[0]
    xp = jnp.where(row == 0, prev_row, pltpu.roll(x, 1, 0))
    xn = jnp.where(row == tm - 1, next_row, pltpu.roll(x, tm - 1, 0))
    return w[0:1] * xp + w[1:2] * x + w[2:3] * xn


def _halo_specs(tm, width, n_rows):
    per = tm // SUBLANES
    last = n_rows // SUBLANES - 1
    prev = pl.BlockSpec((1, SUBLANES, width), lambda bi, i: (bi, jnp.maximum(i * per - 1, 0), 0))
    nxt = pl.BlockSpec((1, SUBLANES, width), lambda bi, i: (bi, jnp.minimum((i + 1) * per, last), 0))
    return prev, nxt


def _gdn_prep_kernel(x_ref, prev_ref, next_ref, ab_ref, cw_ref, par_ref,
                     q_out, k_out, v_out, gb_out, gr_out, *, n_tiles):
    i = pl.program_id(1)
    tm = x_ref.shape[1]
    row = lax.broadcasted_iota(jnp.int32, (tm, 1), 0)
    has_prev = i > 0
    has_next = i < n_tiles - 1
    for j in range(GDN_QKV_DIM // LANES):
        cols = slice(j * LANES, (j + 1) * LANES)
        prev_row = jnp.where(has_prev, prev_ref[0, SUBLANES - 1:SUBLANES, cols], 0.0)
        next_row = jnp.where(has_next, next_ref[0, 0:1, cols], 0.0)
        y = _silu(_conv3_rows(x_ref[0, :, cols], prev_row, next_row, cw_ref[:, cols], row))
        if j < GDN_HEADS:
            y = y * lax.rsqrt(jnp.sum(y * y, axis=-1, keepdims=True) + EPS) * GDN_DK ** -0.5
            q_out[0, :, cols] = y
        elif j < 2 * GDN_HEADS:
            y = y * lax.rsqrt(jnp.sum(y * y, axis=-1, keepdims=True) + EPS)
            k_out[0, :, j * LANES - GDN_QK_DIM:(j + 1) * LANES - GDN_QK_DIM] = y
        else:
            v_out[0, :, j * LANES - 2 * GDN_QK_DIM:(j + 1) * LANES - 2 * GDN_QK_DIM] = y

    ab = ab_ref[0]
    lane = lax.broadcasted_iota(jnp.int32, (tm, LANES), 1)
    t = ab + par_ref[1:2]
    softplus = jnp.maximum(t, 0.0) + jnp.log1p(jnp.exp(-jnp.abs(t)))
    g = -jnp.exp(par_ref[0:1]) * softplus
    beta = _sigmoid(ab)
    pos = row & (CHUNK - 1)
    pre = g
    suf = g
    s = 1
    while s < CHUNK:
        pre = pre + jnp.where(pos >= s, pltpu.roll(pre, s, 0), 0.0)
        suf = suf + jnp.where(pos < CHUNK - s, pltpu.roll(suf, tm - s, 0), 0.0)
        s *= 2
    gb = jnp.where(lane < GDN_HEADS, pre, jnp.where(lane < N_UNITS, suf, beta))
    gb_out[0] = gb
    gr_out[0] = gb.T[0:2 * N_UNITS]


def _gdn_prep(qkv, ab, conv_w, a_log, dt_bias, tm):
    b, l, _ = qkv.shape
    n = l // tm
    par = jnp.zeros((SUBLANES, LANES), F32)
    par = par.at[0, :N_UNITS].set(a_log.reshape(-1)).at[1, :N_UNITS].set(dt_bias.reshape(-1))
    prev, nxt = _halo_specs(tm, GDN_QKV_DIM, l)
    seq = lambda width: pl.BlockSpec((1, tm, width), lambda bi, i: (bi, i, 0))
    return pl.pallas_call(
        functools.partial(_gdn_prep_kernel, n_tiles=n),
        grid=(b, n),
        in_specs=[seq(GDN_QKV_DIM), prev, nxt, seq(LANES),
                  _resident((3, GDN_QKV_DIM)), _resident((SUBLANES, LANES))],
        out_specs=[seq(GDN_QK_DIM), seq(GDN_QK_DIM), seq(GDN_V_DIM), seq(LANES),
                   pl.BlockSpec((1, 2 * N_UNITS, tm), lambda bi, i: (bi, 0, i))],
        out_shape=[jax.ShapeDtypeStruct((b, l, GDN_QK_DIM), F32),
                   jax.ShapeDtypeStruct((b, l, GDN_QK_DIM), F32),
                   jax.ShapeDtypeStruct((b, l, GDN_V_DIM), F32),
                   jax.ShapeDtypeStruct((b, l, LANES), F32),
                   jax.ShapeDtypeStruct((b, 2 * N_UNITS, l), F32)],
        compiler_params=_params("parallel", "arbitrary"),
        name="gdn_prep",
    )(qkv, qkv, qkv, ab, conv_w, par)


def _unit_triangular_inverses(a_list, eye, diag_blocks, level_masks):
    ad = [jnp.where(diag_blocks, a, 0.0) for a in a_list]
    a2 = [_dot(x, x) for x in ad]
    a4 = [_dot(x, x) for x in a2]
    t = [_dot(eye - x, eye + y) for x, y in zip(ad, a2)]
    a8 = [_dot(x, x) for x in a4]
    t = [_dot(x, eye + y) for x, y in zip(t, a4)]
    t = [_dot(x, eye + y) for x, y in zip(t, a8)]
    for m in level_masks:
        at = [_dot(jnp.where(m, a, 0.0), x) for a, x in zip(a_list, t)]
        t = [x - _dot(x, y) for x, y in zip(t, at)]
    return t


def _gdn_scan_kernel(qf_ref, kf_ref, vf_ref, gbf_ref, grf_ref,
                     qb_ref, kb_ref, vb_ref, gbb_ref, grb_ref, s0_ref,
                     of_ref, ob_ref, s_out, s_scr, *, n_blocks):
    i = pl.program_id(1)

    @pl.when(i == 0)
    def _():
        s_scr[...] = s0_ref[0]

    row = lax.broadcasted_iota(jnp.int32, (CHUNK, CHUNK), 0)
    col = lax.broadcasted_iota(jnp.int32, (CHUNK, CHUNK), 1)
    eye = jnp.where(row == col, 1.0, 0.0)
    same = lambda shift: (row >> shift) == (col >> shift)
    diag_blocks = same(4)
    level_masks = [same(sh) & jnp.logical_not(same(sh - 1)) for sh in (5, 6, 7)]

    units = []
    for d, refs in enumerate(((qf_ref, kf_ref, vf_ref, gbf_ref, grf_ref, of_ref),
                              (qb_ref, kb_ref, vb_ref, gbb_ref, grb_ref, ob_ref))):
        for h in range(GDN_HEADS):
            units.append((d, h, d * GDN_HEADS + h, slice(h * GDN_DK, (h + 1) * GDN_DK)) + refs)

    n_chunks = qf_ref.shape[1] // CHUNK
    q, k, v, g_col, beta, g_last, decay, kq, a = ({} for _ in range(9))
    for c in range(n_chunks):
        rows = slice(c * CHUNK, (c + 1) * CHUNK)
        for d, h, u, cols, q_ref, k_ref, v_ref, gb_ref, gr_ref, o_ref in units:
            incl = (row >= col) if d == 0 else (row <= col)
            last = CHUNK - 1 if d == 0 else 0
            key = (c, u)
            q[key] = q_ref[0, rows, cols]
            k[key] = k_ref[0, rows, cols]
            v[key] = v_ref[0, rows, cols]
            g_col[key] = gb_ref[0, rows, u:u + 1]
            beta[key] = gb_ref[0, rows, N_UNITS + u:N_UNITS + u + 1]
            g_last[key] = g_col[key][last:last + 1, :]
            decay[key] = jnp.exp(jnp.where(incl, g_col[key] - gr_ref[0, u:u + 1, rows], -jnp.inf))
            kq[key] = lax.dot_general(jnp.concatenate([k[key], q[key]], axis=0).astype(BF16),
                                      k[key].astype(BF16), _NT, preferred_element_type=F32)
    keys = list(kq)
    for key in keys:
        strict = (row > col) if key[1] < GDN_HEADS else (row < col)
        a[key] = jnp.where(strict, beta[key] * kq[key][:CHUNK] * decay[key], 0.0)
    t_mat = dict(zip(keys, _unit_triangular_inverses([a[key] for key in keys], eye, diag_blocks, level_masks)))

    state = [s_scr[u] for u in range(N_UNITS)]
    for step in range(n_chunks):
        order = [(step if unit[0] == 0 else n_chunks - 1 - step, unit[2]) for unit in units]
        ps = [_dot(jnp.concatenate([k[key] * jnp.exp(g_col[key]), q[key] * jnp.exp(g_col[key])], axis=0),
                   state[key[1]]) for key in order]
        v_new = [_dot(t_mat[key], beta[key] * (v[key] - ps_u[:CHUNK])) for key, ps_u in zip(order, ps)]
        for unit, key, ps_u, v_new_u in zip(units, order, ps, v_new):
            cols, o_ref = unit[3], unit[9]
            o_ref[0, key[0] * CHUNK:(key[0] + 1) * CHUNK, cols] = (
                ps_u[CHUNK:] + _dot(kq[key][CHUNK:] * decay[key], v_new_u))
        for key, v_new_u in zip(order, v_new):
            k_tail = k[key] * jnp.exp(g_last[key] - g_col[key])
            state[key[1]] = state[key[1]] * jnp.exp(g_last[key]) + _dot(k_tail.T, v_new_u)
    for u in range(N_UNITS):
        s_scr[u] = state[u]

    @pl.when(i == n_blocks - 1)
    def _():
        s_out[0] = s_scr[...]


def _gdn_scan(qn, kn, vv, gb, gr, state0, rows=SCAN_ROWS):
    b, l, _ = qn.shape
    n = l // rows
    fwd = lambda width: pl.BlockSpec((1, rows, width), lambda bi, i: (bi, i, 0))
    bwd = lambda width: pl.BlockSpec((1, rows, width), lambda bi, i: (bi, n - 1 - i, 0))
    gr_f = pl.BlockSpec((1, 2 * N_UNITS, rows), lambda bi, i: (bi, 0, i))
    gr_b = pl.BlockSpec((1, 2 * N_UNITS, rows), lambda bi, i: (bi, 0, n - 1 - i))
    st = pl.BlockSpec((1, N_UNITS, GDN_DK, GDN_DK), lambda bi, i: (bi, 0, 0, 0))
    return pl.pallas_call(
        functools.partial(_gdn_scan_kernel, n_blocks=n),
        grid=(b, n),
        in_specs=[fwd(GDN_QK_DIM), fwd(GDN_QK_DIM), fwd(GDN_V_DIM), fwd(LANES), gr_f,
                  bwd(GDN_QK_DIM), bwd(GDN_QK_DIM), bwd(GDN_V_DIM), bwd(LANES), gr_b, st],
        out_specs=[fwd(GDN_V_DIM), bwd(GDN_V_DIM), st],
        out_shape=[jax.ShapeDtypeStruct((b, l, GDN_V_DIM), F32),
                   jax.ShapeDtypeStruct((b, l, GDN_V_DIM), F32),
                   jax.ShapeDtypeStruct((b, N_UNITS, GDN_DK, GDN_DK), F32)],
        scratch_shapes=[pltpu.VMEM((N_UNITS, GDN_DK, GDN_DK), F32)],
        compiler_params=_params("parallel", "arbitrary"),
        name="gdn_scan",
    )(qn, kn, vv, gb, gr, qn, kn, vv, gb, gr, state0)


def _attn_kernel(q_ref, kvp_ref, kvc_ref, kvn_ref, ctx_ref, sink_ref, o_ref, *, n_blocks, n_ctx_tiles):
    i = pl.program_id(1)
    row = lax.broadcasted_iota(jnp.int32, (WINDOW, WINDOW), 0)
    col = lax.broadcasted_iota(jnp.int32, (WINDOW, WINDOW), 1)
    mask_prev = (col >= row) & (i > 0)
    mask_next = (col <= row) & (i < n_blocks - 1)
    def tiles(head, base):
        kv_head = head // (ATTN_Q_HEADS // ATTN_KV_HEADS)
        c = (base + 2 * kv_head + head % 2) * LANES
        out = [kvp_ref[0, :, c:c + LANES], kvc_ref[0, :, c:c + LANES], kvn_ref[0, :, c:c + LANES]]
        return out + [ctx_ref[0, t * WINDOW:(t + 1) * WINDOW, c:c + LANES] for t in range(n_ctx_tiles)]

    n_pairs = ATTN_Q_HEADS // 2
    left = lax.broadcasted_iota(jnp.int32, (WINDOW, LANES), 1) < HEAD_DIM

    def pair_scores(pair):
        q = q_ref[0, :, pair * LANES:(pair + 1) * LANES]
        both = [lax.dot_general(q, jnp.concatenate([ka, kb], axis=0), _NT, preferred_element_type=F32)
                for ka, kb in zip(tiles(2 * pair, 0), tiles(2 * pair + 1, 0))]
        return [t[:, :WINDOW] for t in both], [t[:, WINDOW:] for t in both]

    def softmax_terms(head, s):
        sink = sink_ref[head]
        s = [jnp.where(mask_prev, s[0], -jnp.inf), s[1], jnp.where(mask_next, s[2], -jnp.inf)] + s[3:]
        m = s[1]
        for st in s[:1] + s[2:]:
            m = jnp.maximum(m, st)
        m = jnp.maximum(jnp.max(m, axis=-1, keepdims=True), sink)
        e = [jnp.exp(st - m) for st in s]
        tot = e[0]
        for et in e[1:]:
            tot = tot + et
        inv_denom = 1.0 / (jnp.sum(tot, axis=-1, keepdims=True) + jnp.exp(sink - m))
        return [et.astype(BF16) for et in e], inv_denom

    def pair_output(pair, terms_a, terms_b):
        pv = None
        for ea, eb, va, vb in zip(terms_a[0], terms_b[0], tiles(2 * pair, 4), tiles(2 * pair + 1, 4)):
            part = jnp.dot(jnp.concatenate([ea, eb], axis=1), jnp.concatenate([va, vb], axis=0),
                           preferred_element_type=F32)
            pv = part if pv is None else pv + part
        acc = pv * jnp.where(left, terms_a[1], terms_b[1])
        o_ref[0, :, pair * LANES:(pair + 1) * LANES] = acc.astype(BF16)

    scores = [pair_scores(pair) for pair in range(n_pairs)]
    terms = [(softmax_terms(2 * pair, s[0]), softmax_terms(2 * pair + 1, s[1])) for pair, s in enumerate(scores)]
    for pair, (terms_a, terms_b) in enumerate(terms):
        pair_output(pair, terms_a, terms_b)


def _attn(q, kvm, kvm_ctx, sink):
    b, l, _ = q.shape
    lc = kvm_ctx.shape[1]
    n = l // WINDOW
    blk = lambda f: pl.BlockSpec((1, WINDOW, KVM_WIDTH), f)
    return pl.pallas_call(
        functools.partial(_attn_kernel, n_blocks=n, n_ctx_tiles=lc // WINDOW),
        grid=(b, n),
        in_specs=[pl.BlockSpec((1, WINDOW, ATTN_Q_DIM), lambda bi, i: (bi, i, 0)),
                  blk(lambda bi, i: (bi, jnp.maximum(i - 1, 0), 0)),
                  blk(lambda bi, i: (bi, i, 0)),
                  blk(lambda bi, i: (bi, jnp.minimum(i + 1, n - 1), 0)),
                  pl.BlockSpec((1, lc, KVM_WIDTH), lambda bi, i: (bi, 0, 0)),
                  pl.BlockSpec(memory_space=pltpu.SMEM)],
        out_specs=pl.BlockSpec((1, WINDOW, ATTN_Q_DIM), lambda bi, i: (bi, i, 0)),
        out_shape=jax.ShapeDtypeStruct((b, l, ATTN_Q_DIM), BF16),
        compiler_params=_params("parallel", "arbitrary"),
        name="attn",
    )(q, kvm, kvm, kvm, kvm_ctx, sink)


def _row_parts(tm):
    part = tm // FFN_PARTS
    return [slice(r * part, (r + 1) * part) for r in range(FFN_PARTS)]


def _residual_ffn(xs, ys, mod, gains, wg_ref, wu_ref, wd_ref):
    x1 = [x + mod[2:3] * (_rms(y) * gains[0:1]) for x, y in zip(xs, ys)]
    h = [_modulated_norm(t, gains[1:2], mod, 3).astype(BF16) for t in x1]
    gate_up = [(jnp.dot(t, wg_ref[...], preferred_element_type=F32),
                jnp.dot(t, wu_ref[...], preferred_element_type=F32)) for t in h]
    f = [_dot(_silu(gate) * up, wd_ref[...]) for gate, up in gate_up]
    return [t + mod[5:6] * (_rms(ft) * gains[2:3]) for t, ft in zip(x1, f)]


def _post_hy_kernel(x_ref, attn_ref, of_ref, ob_ref, z_ref, mod_ref, gains_ref, ng_ref,
                    wo_ref, wg_ref, wu_ref, wd_ref, out_ref):
    parts = _row_parts(x_ref.shape[1])
    ys = []
    for rows in parts:
        y = jnp.dot(attn_ref[0, rows, :], wo_ref[0:ATTN_Q_DIM, :], preferred_element_type=F32)
        for h in range(GDN_HEADS):
            cols = slice(h * GDN_DK, (h + 1) * GDN_DK)
            gated = (_rms(of_ref[0, rows, cols] + ob_ref[0, rows, cols]) * ng_ref[...]
                     * _silu(z_ref[0, rows, cols]))
            y = y + _dot(gated, wo_ref[ATTN_Q_DIM + h * GDN_DK:ATTN_Q_DIM + (h + 1) * GDN_DK, :])
        ys.append(y)
    outs = _residual_ffn([x_ref[0, rows, :] for rows in parts], ys, mod_ref[0], gains_ref[...],
                         wg_ref, wu_ref, wd_ref)
    for rows, out in zip(parts, outs):
        out_ref[0, rows, :] = out


def _post_hy(x, attn, o_f, o_b, z, mod, gains, norm_g, w_out, w_gate, w_up, w_down, tm):
    b, l, d = x.shape
    hidden = w_gate.shape[1]
    seq = lambda width: pl.BlockSpec((1, tm, width), lambda bi, i: (bi, i, 0))
    return pl.pallas_call(
        _post_hy_kernel,
        grid=(b, l // tm),
        in_specs=[seq(d), seq(ATTN_Q_DIM), seq(GDN_V_DIM), seq(GDN_V_DIM), seq(GDN_V_DIM),
                  pl.BlockSpec((1, 6, d), lambda bi, i: (bi, 0, 0)),
                  _resident((3, d)), _resident((1, GDN_DK)),
                  _resident((ATTN_Q_DIM + GDN_V_DIM, d)),
                  _resident((d, hidden)), _resident((d, hidden)), _resident((hidden, d))],
        out_specs=seq(d),
        out_shape=jax.ShapeDtypeStruct((b, l, d), F32),
        compiler_params=_params("parallel", "arbitrary"),
        name="post_hy",
    )(x, attn, o_f, o_b, z, mod, gains, norm_g.reshape(1, GDN_DK), w_out, w_gate, w_up, w_down)


def _inproj_sc_kernel(x_ref, mod_ref, g_ref, w_ref, b_out, cu_out):
    d = x_ref.shape[2]
    parts = _row_parts(x_ref.shape[1])
    hs = [_modulated_norm(x_ref[0, rows, :], g_ref[...], mod_ref[0], 0) for rows in parts]
    ps = [_dot(h, w_ref[...]) for h in hs]
    for rows, p in zip(parts, ps):
        b_out[0, rows, :] = p[:, 0:d]
        cu_out[0, rows, :] = p[:, d:2 * d] * p[:, 2 * d:3 * d]


def _inproj_sc(x, mod, gain, w_in, tm):
    b, l, d = x.shape
    seq = pl.BlockSpec((1, tm, d), lambda bi, i: (bi, i, 0))
    return pl.pallas_call(
        _inproj_sc_kernel,
        grid=(b, l // tm),
        in_specs=[seq, pl.BlockSpec((1, 6, d), lambda bi, i: (bi, 0, 0)),
                  _resident((1, d)), _resident((d, 3 * d))],
        out_specs=[seq, seq],
        out_shape=[jax.ShapeDtypeStruct((b, l, d), F32), jax.ShapeDtypeStruct((b, l, d), F32)],
        compiler_params=_params("parallel", "arbitrary"),
        name="inproj_sc",
    )(x, mod, gain.reshape(1, d), w_in)


def _post_sc_kernel(x_ref, b_ref, cu_ref, prev_ref, next_ref, mod_ref, gains_ref, cw_ref,
                    wo_ref, wg_ref, wu_ref, wd_ref, out_ref, *, n_tiles):
    i = pl.program_id(1)
    tm = x_ref.shape[1]
    row = lax.broadcasted_iota(jnp.int32, (tm, 1), 0)
    prev_row = jnp.where(i > 0, prev_ref[0, SUBLANES - 1:SUBLANES, :], 0.0)
    next_row = jnp.where(i < n_tiles - 1, next_ref[0, 0:1, :], 0.0)
    conv = _conv3_rows(cu_ref[0], prev_row, next_row, cw_ref[...], row)
    parts = _row_parts(tm)
    ys = [_dot(b_ref[0, rows, :] * conv[rows], wo_ref[...]) for rows in parts]
    outs = _residual_ffn([x_ref[0, rows, :] for rows in parts], ys, mod_ref[0], gains_ref[...],
                         wg_ref, wu_ref, wd_ref)
    for rows, out in zip(parts, outs):
        out_ref[0, rows, :] = out


def _post_sc(x, b_gate, cu, mod, gains, conv_w, w_out, w_gate, w_up, w_down, tm):
    b, l, d = x.shape
    hidden = w_gate.shape[1]
    n = l // tm
    seq = pl.BlockSpec((1, tm, d), lambda bi, i: (bi, i, 0))
    prev, nxt = _halo_specs(tm, d, l)
    return pl.pallas_call(
        functools.partial(_post_sc_kernel, n_tiles=n),
        grid=(b, n),
        in_specs=[seq, seq, seq, prev, nxt,
                  pl.BlockSpec((1, 6, d), lambda bi, i: (bi, 0, 0)),
                  _resident((3, d)), _resident((3, d)), _resident((d, d)),
                  _resident((d, hidden)), _resident((d, hidden)), _resident((hidden, d))],
        out_specs=seq,
        out_shape=jax.ShapeDtypeStruct((b, l, d), F32),
        compiler_params=_params("parallel", "arbitrary"),
        name="post_sc",
    )(x, b_gate, cu, cu, cu, mod, gains, conv_w, w_out, w_gate, w_up, w_down)


def _rope_tables(length):
    pos = jnp.arange(length)
    n_freq = HEAD_DIM // 4
    inv_freq = ROPE_BASE ** (-jnp.arange(n_freq, dtype=F32) / n_freq)
    ang = jnp.concatenate([(pos // GRID_W).astype(F32)[:, None] * inv_freq,
                           (pos % GRID_W).astype(F32)[:, None] * inv_freq], axis=-1)
    cos, sin = jnp.cos(ang), jnp.sin(ang)
    return jnp.tile(cos, (1, 4)), jnp.tile(jnp.concatenate([-sin, sin], axis=-1), (1, 2))


def _row_tile(length, target):
    tm = min(length, target)
    assert length % tm == 0 and tm % CHUNK == 0
    return tm


def kernel(x, c, ctx, c_ctx, ada_w, ada_b, pre_mix_g, post_mix_g, pre_ffn_g, post_ffn_g, hy_w_in, hy_w_out,
           attn_sink, gdn_conv_w, gdn_a_log, gdn_dt_bias, gdn_norm_g, sc_w_in, sc_conv_w, sc_w_out,
           ffn_w_gate, ffn_w_up, ffn_w_down):
    b, l, d = x.shape
    lc = ctx.shape[1]
    assert ada_w.shape[0] == 2 and l % CHUNK == 0 and lc % CHUNK == 0 and b + 1 <= SUBLANES
    tm = _row_tile(l, 512)
    tmc = _row_tile(lc, 256)

    cond = jnp.concatenate([c, c_ctx[None], jnp.zeros((SUBLANES - b - 1, d), F32)], axis=0)
    mod = _ada(cond, ada_w, ada_b).reshape(2, SUBLANES, 6, d)
    gains = jnp.stack([post_mix_g, pre_ffn_g, post_ffn_g], axis=1)
    bf = lambda w: w.astype(BF16)

    w_in = bf(jnp.pad(hy_w_in[0], ((0, 0), (0, HY_IN_PAD - hy_w_in.shape[2]))))
    cos, sin = _rope_tables(l)
    q, kvm, qkv, z, ab = _inproj_hy(x, mod[0], None, pre_mix_g[0], w_in, cos, sin, tm)
    ones, zeros = jnp.ones((lc, LANES), F32), jnp.zeros((lc, LANES), F32)
    _, kvm_c, qkv_c, _, ab_c = _inproj_hy(ctx, mod[0], b, pre_mix_g[0], w_in, ones, zeros, tmc)

    lat = _gdn_prep(qkv, ab, gdn_conv_w[0], gdn_a_log[0], gdn_dt_bias[0], tm)
    con = _gdn_prep(qkv_c, ab_c, gdn_conv_w[0], gdn_a_log[0], gdn_dt_bias[0], tmc)
    state0 = jnp.zeros((b, N_UNITS, GDN_DK, GDN_DK), F32)
    _, _, state_c = _gdn_scan(*con, state0, _row_tile(lc, SCAN_ROWS))
    o_f, o_b, _ = _gdn_scan(*lat, state_c, _row_tile(l, SCAN_ROWS))

    attn = _attn(q, kvm, kvm_c, attn_sink[0])
    x = _post_hy(x, attn, o_f, o_b, z, mod[0], gains[0], gdn_norm_g[0], bf(hy_w_out[0]),
                 bf(ffn_w_gate[0]), bf(ffn_w_up[0]), bf(ffn_w_down[0]), tm)

    b_gate, cu = _inproj_sc(x, mod[1], pre_mix_g[1], bf(sc_w_in[0]), tm)
    return _post_sc(x, b_gate, cu, mod[1], gains[1], sc_conv_w[0], bf(sc_w_out[0]),
                    bf(ffn_w_gate[1]), bf(ffn_w_up[1]), bf(ffn_w_down[1]), tm)
```

```python
import functools

import jax
import jax.numpy as jnp
from jax import lax
from jax.experimental import pallas as pl
from jax.experimental.pallas import tpu as pltpu

F32 = jnp.float32
BF16 = jnp.bfloat16

EPS = 1e-6
GRID_W = 64
ROPE_BASE = 10000.0
HEAD_DIM = 64
ATTN_Q_HEADS = 8
ATTN_KV_HEADS = 2
ATTN_Q_DIM = ATTN_Q_HEADS * HEAD_DIM
ATTN_KV_DIM = ATTN_KV_HEADS * HEAD_DIM
WINDOW = 128
GDN_HEADS = 4
GDN_DK = 128
GDN_QK_DIM = GDN_HEADS * GDN_DK
GDN_V_DIM = GDN_HEADS * GDN_DK
GDN_QKV_DIM = 2 * GDN_QK_DIM + GDN_V_DIM
N_UNITS = 2 * GDN_HEADS
LANES = 128
SUBLANES = 8
CHUNK = 128
SCAN_ROWS = 4 * CHUNK
FFN_PARTS = 2
COL_Q = 0
COL_K = COL_Q + ATTN_Q_DIM
COL_V = COL_K + ATTN_KV_DIM
COL_QKV = COL_V + ATTN_KV_DIM
COL_Z = COL_QKV + GDN_QKV_DIM
COL_AB = COL_Z + GDN_V_DIM
HY_IN_PAD = COL_AB + LANES
KVM_WIDTH = 8 * LANES
VMEM_LIMIT = 56 * 1024 * 1024

_NT = (((1,), (1,)), ((), ()))


def _sigmoid(x):
    return 1.0 / (1.0 + jnp.exp(-x))


def _silu(x):
    return x * _sigmoid(x)


def _rms(x):
    return x * lax.rsqrt(jnp.mean(x * x, axis=-1, keepdims=True) + EPS)


def _dot(a, b):
    return jnp.dot(a.astype(BF16), b.astype(BF16), preferred_element_type=F32)


def _params(*sem):
    return pltpu.CompilerParams(dimension_semantics=sem, vmem_limit_bytes=VMEM_LIMIT)


def _resident(shape):
    zeros = (0,) * len(shape)
    return pl.BlockSpec(shape, lambda *_: zeros, pipeline_mode=pl.Buffered(1))


def _ada_kernel(cond_ref, w_ref, b_ref, out_ref):
    s = _silu(cond_ref[...])
    out_ref[0] = jnp.dot(s, w_ref[0], precision=lax.Precision.HIGHEST,
                         preferred_element_type=F32) + b_ref[0]


def _ada(cond, ada_w, ada_b, tn=1536):
    depth, d, n = ada_w.shape
    rows = cond.shape[0]
    return pl.pallas_call(
        _ada_kernel,
        grid=(depth, n // tn),
        in_specs=[pl.BlockSpec((rows, d), lambda l, j: (0, 0)),
                  pl.BlockSpec((1, d, tn), lambda l, j: (l, 0, j)),
                  pl.BlockSpec((1, 1, tn), lambda l, j: (l, 0, j))],
        out_specs=pl.BlockSpec((1, rows, tn), lambda l, j: (l, 0, j)),
        out_shape=jax.ShapeDtypeStruct((depth, rows, n), F32),
        compiler_params=_params("arbitrary", "arbitrary"),
        name="ada",
    )(cond, ada_w, ada_b.reshape(depth, 1, n))


def _modulated_norm(x, gain, mod, shift_row):
    return _rms(x) * gain * (1.0 + mod[shift_row + 1:shift_row + 2]) + mod[shift_row:shift_row + 1]


def _inproj_hy_kernel(x_ref, xp_ref, xn_ref, mod_ref, g_ref, w_ref, cos_ref, sin_ref, cw_ref, par_ref,
                      q_out, kvm_out, qn_out, kn_out, vv_out, z_out, gb_out, gr_out, *, n_tiles):
    i = pl.program_id(1)
    tm = x_ref.shape[1]
    ext = tm + 2 * SUBLANES
    main = slice(SUBLANES, SUBLANES + tm)
    x_ext = jnp.concatenate([xp_ref[0], x_ref[0], xn_ref[0]], axis=0)
    p_ext = _dot(_modulated_norm(x_ext, g_ref[...], mod_ref[0], 0), w_ref[...])
    p = p_ext[main]

    cos = cos_ref[...]
    sin = sin_ref[...]
    lane = lax.broadcasted_iota(jnp.int32, (tm, LANES), 1)
    first_half = (lane & (HEAD_DIM - 1)) < HEAD_DIM // 2
    left = lane < HEAD_DIM

    def rope(t):
        partner = jnp.where(first_half, pltpu.roll(t, LANES - HEAD_DIM // 2, 1),
                            pltpu.roll(t, HEAD_DIM // 2, 1))
        return t * cos + partner * sin

    for j in range(ATTN_Q_DIM // LANES):
        c0 = COL_Q + j * LANES
        q_out[0, :, j * LANES:(j + 1) * LANES] = (rope(p[:, c0:c0 + LANES]) * HEAD_DIM ** -0.5).astype(BF16)

    def masked_layouts(t):
        sw = pltpu.roll(t, HEAD_DIM, 1)
        return (jnp.where(left, t, 0.0), jnp.where(left, 0.0, sw),
                jnp.where(left, sw, 0.0), jnp.where(left, 0.0, t))

    k_lay = masked_layouts(rope(p[:, COL_K:COL_K + LANES]))
    v_lay = masked_layouts(p[:, COL_V:COL_V + LANES])
    for j, t in enumerate(k_lay + v_lay):
        kvm_out[0, :, j * LANES:(j + 1) * LANES] = t.astype(BF16)
    z_out[0] = p[:, COL_Z:COL_AB]

    has_prev = i > 0
    has_next = i < n_tiles - 1
    for j in range(GDN_QKV_DIM // LANES):
        c0 = COL_QKV + j * LANES
        cols = slice(j * LANES, (j + 1) * LANES)
        t = jnp.concatenate([jnp.where(has_prev, p_ext[0:SUBLANES, c0:c0 + LANES], 0.0),
                             p[:, c0:c0 + LANES],
                             jnp.where(has_next, p_ext[SUBLANES + tm:ext, c0:c0 + LANES], 0.0)], axis=0)
        y = (cw_ref[0:1, cols] * pltpu.roll(t, 1, 0) + cw_ref[1:2, cols] * t
             + cw_ref[2:3, cols] * pltpu.roll(t, ext - 1, 0))
        y = _silu(y[main])
        if j < GDN_HEADS:
            y = y * lax.rsqrt(jnp.sum(y * y, axis=-1, keepdims=True) + EPS) * GDN_DK ** -0.5
            qn_out[0, :, cols] = y
        elif j < 2 * GDN_HEADS:
            y = y * lax.rsqrt(jnp.sum(y * y, axis=-1, keepdims=True) + EPS)
            kn_out[0, :, j * LANES - GDN_QK_DIM:(j + 1) * LANES - GDN_QK_DIM] = y
        else:
            vv_out[0, :, j * LANES - 2 * GDN_QK_DIM:(j + 1) * LANES - 2 * GDN_QK_DIM] = y

    ab = p[:, COL_AB:HY_IN_PAD]
    t = ab + par_ref[1:2]
    softplus = jnp.maximum(t, 0.0) + jnp.log1p(jnp.exp(-jnp.abs(t)))
    g = -jnp.exp(par_ref[0:1]) * softplus
    beta = _sigmoid(ab)
    pos = lax.broadcasted_iota(jnp.int32, (tm, 1), 0) & (CHUNK - 1)
    pre = g
    suf = g
    s = 1
    while s < CHUNK:
        pre = pre + jnp.where(pos >= s, pltpu.roll(pre, s, 0), 0.0)
        suf = suf + jnp.where(pos < CHUNK - s, pltpu.roll(suf, tm - s, 0), 0.0)
        s *= 2
    gb = jnp.where(lane < GDN_HEADS, pre, jnp.where(lane < N_UNITS, suf, beta))
    gb_out[0] = gb
    gr_out[0] = gb.T[0:2 * N_UNITS]


def _inproj_hy(x, mod, mod_row, gain, w_pad, cos, sin, conv_w, a_log, dt_bias, tm):
    b, l, d = x.shape
    n = l // tm
    row_of = (lambda bi: bi) if mod_row is None else (lambda bi: mod_row)
    seq = lambda width: pl.BlockSpec((1, tm, width), lambda bi, i: (bi, i, 0))
    prev, nxt = _halo_specs(tm, d, l)
    par = jnp.zeros((SUBLANES, LANES), F32)
    par = par.at[0, :N_UNITS].set(a_log.reshape(-1)).at[1, :N_UNITS].set(dt_bias.reshape(-1))
    return pl.pallas_call(
        functools.partial(_inproj_hy_kernel, n_tiles=n),
        grid=(b, n),
        in_specs=[seq(d), prev, nxt,
                  pl.BlockSpec((1, 6, d), lambda bi, i: (row_of(bi), 0, 0)),
                  _resident((1, d)),
                  _resident((d, HY_IN_PAD)),
                  pl.BlockSpec((tm, LANES), lambda bi, i: (i, 0)),
                  pl.BlockSpec((tm, LANES), lambda bi, i: (i, 0)),
                  _resident((3, GDN_QKV_DIM)), _resident((SUBLANES, LANES))],
        out_specs=[seq(ATTN_Q_DIM), seq(KVM_WIDTH), seq(GDN_QK_DIM), seq(GDN_QK_DIM), seq(GDN_V_DIM),
                   seq(GDN_V_DIM), seq(LANES),
                   pl.BlockSpec((1, 2 * N_UNITS, tm), lambda bi, i: (bi, 0, i))],
        out_shape=[jax.ShapeDtypeStruct((b, l, ATTN_Q_DIM), BF16),
                   jax.ShapeDtypeStruct((b, l, KVM_WIDTH), BF16),
                   jax.ShapeDtypeStruct((b, l, GDN_QK_DIM), F32),
                   jax.ShapeDtypeStruct((b, l, GDN_QK_DIM), F32),
                   jax.ShapeDtypeStruct((b, l, GDN_V_DIM), F32),
                   jax.ShapeDtypeStruct((b, l, GDN_V_DIM), F32),
                   jax.ShapeDtypeStruct((b, l, LANES), F32),
                   jax.ShapeDtypeStruct((b, 2 * N_UNITS, l), F32)],
        compiler_params=_params("parallel", "arbitrary"),
        name="inproj_hy",
    )(x, x, x, mod, gain.reshape(1, d), w_pad, cos, sin, conv_w, par)


def _conv3_rows(x, prev_row, next_row, w, row):
    tm = x.shape[0]
    xp = jnp.where(row == 0, prev_row, pltpu.roll(x, 1, 0))
    xn = jnp.where(row == tm - 1, next_row, pltpu.roll(x, tm - 1, 0))
    return w[0:1] * xp + w[1:2] * x + w[2:3] * xn


def _halo_specs(tm, width, n_rows):
    per = tm // SUBLANES
    last = n_rows // SUBLANES - 1
    prev = pl.BlockSpec((1, SUBLANES, width), lambda bi, i: (bi, jnp.maximum(i * per - 1, 0), 0))
    nxt = pl.BlockSpec((1, SUBLANES, width), lambda bi, i: (bi, jnp.minimum((i + 1) * per, last), 0))
    return prev, nxt


def _unit_triangular_inverses(a_list, eye, diag_blocks, level_masks):
    ad = [jnp.where(diag_blocks, a, 0.0) for a in a_list]
    a2 = [_dot(x, x) for x in ad]
    a4 = [_dot(x, x) for x in a2]
    t = [_dot(eye - x, eye + y) for x, y in zip(ad, a2)]
    a8 = [_dot(x, x) for x in a4]
    t = [_dot(x, eye + y) for x, y in zip(t, a4)]
    t = [_dot(x, eye + y) for x, y in zip(t, a8)]
    for m in level_masks:
        at = [_dot(jnp.where(m, a, 0.0), x) for a, x in zip(a_list, t)]
        t = [x - _dot(x, y) for x, y in zip(t, at)]
    return t


def _gdn_scan_kernel(qf_ref, kf_ref, vf_ref, gbf_ref, grf_ref,
                     qb_ref, kb_ref, vb_ref, gbb_ref, grb_ref, s0_ref,
                     of_ref, ob_ref, s_out, s_scr, *, n_blocks):
    i = pl.program_id(1)

    @pl.when(i == 0)
    def _():
        s_scr[...] = s0_ref[0]

    row = lax.broadcasted_iota(jnp.int32, (CHUNK, CHUNK), 0)
    col = lax.broadcasted_iota(jnp.int32, (CHUNK, CHUNK), 1)
    eye = jnp.where(row == col, 1.0, 0.0)
    same = lambda shift: (row >> shift) == (col >> shift)
    diag_blocks = same(4)
    level_masks = [same(sh) & jnp.logical_not(same(sh - 1)) for sh in (5, 6, 7)]

    units = []
    for d, refs in enumerate(((qf_ref, kf_ref, vf_ref, gbf_ref, grf_ref, of_ref),
                              (qb_ref, kb_ref, vb_ref, gbb_ref, grb_ref, ob_ref))):
        for h in range(GDN_HEADS):
            units.append((d, h, d * GDN_HEADS + h, slice(h * GDN_DK, (h + 1) * GDN_DK)) + refs)

    n_chunks = qf_ref.shape[1] // CHUNK
    q, k, v, g_col, beta, g_last, decay, kq, a = ({} for _ in range(9))
    for c in range(n_chunks):
        rows = slice(c * CHUNK, (c + 1) * CHUNK)
        for d, h, u, cols, q_ref, k_ref, v_ref, gb_ref, gr_ref, o_ref in units:
            incl = (row >= col) if d == 0 else (row <= col)
            last = CHUNK - 1 if d == 0 else 0
            key = (c, u)
            q[key] = q_ref[0, rows, cols]
            k[key] = k_ref[0, rows, cols]
            v[key] = v_ref[0, rows, cols]
            g_col[key] = gb_ref[0, rows, u:u + 1]
            beta[key] = gb_ref[0, rows, N_UNITS + u:N_UNITS + u + 1]
            g_last[key] = g_col[key][last:last + 1, :]
            decay[key] = jnp.exp(jnp.where(incl, g_col[key] - gr_ref[0, u:u + 1, rows], -jnp.inf))
            kq[key] = lax.dot_general(jnp.concatenate([k[key], q[key]], axis=0).astype(BF16),
                                      k[key].astype(BF16), _NT, preferred_element_type=F32)
    keys = list(kq)
    for key in keys:
        strict = (row > col) if key[1] < GDN_HEADS else (row < col)
        a[key] = jnp.where(strict, beta[key] * kq[key][:CHUNK] * decay[key], 0.0)
    t_mat = dict(zip(keys, _unit_triangular_inverses([a[key] for key in keys], eye, diag_blocks, level_masks)))

    state = [s_scr[u] for u in range(N_UNITS)]
    for step in range(n_chunks):
        order = [(step if unit[0] == 0 else n_chunks - 1 - step, unit[2]) for unit in units]
        ps = [_dot(jnp.concatenate([k[key] * jnp.exp(g_col[key]), q[key] * jnp.exp(g_col[key])], axis=0),
                   state[key[1]]) for key in order]
        v_new = [_dot(t_mat[key], beta[key] * (v[key] - ps_u[:CHUNK])) for key, ps_u in zip(order, ps)]
        for unit, key, ps_u, v_new_u in zip(units, order, ps, v_new):
            cols, o_ref = unit[3], unit[9]
            o_ref[0, key[0] * CHUNK:(key[0] + 1) * CHUNK, cols] = (
                ps_u[CHUNK:] + _dot(kq[key][CHUNK:] * decay[key], v_new_u))
        for key, v_new_u in zip(order, v_new):
            k_tail = k[key] * jnp.exp(g_last[key] - g_col[key])
            state[key[1]] = state[key[1]] * jnp.exp(g_last[key]) + _dot(k_tail.T, v_new_u)
    for u in range(N_UNITS):
        s_scr[u] = state[u]

    @pl.when(i == n_blocks - 1)
    def _():
        s_out[0] = s_scr[...]


def _gdn_scan(qn, kn, vv, gb, gr, state0, rows=SCAN_ROWS):
    b, l, _ = qn.shape
    n = l // rows
    fwd = lambda width: pl.BlockSpec((1, rows, width), lambda bi, i: (bi, i, 0))
    bwd = lambda width: pl.BlockSpec((1, rows, width), lambda bi, i: (bi, n - 1 - i, 0))
    gr_f = pl.BlockSpec((1, 2 * N_UNITS, rows), lambda bi, i: (bi, 0, i))
    gr_b = pl.BlockSpec((1, 2 * N_UNITS, rows), lambda bi, i: (bi, 0, n - 1 - i))
    st = pl.BlockSpec((1, N_UNITS, GDN_DK, GDN_DK), lambda bi, i: (bi, 0, 0, 0))
    return pl.pallas_call(
        functools.partial(_gdn_scan_kernel, n_blocks=n),
        grid=(b, n),
        in_specs=[fwd(GDN_QK_DIM), fwd(GDN_QK_DIM), fwd(GDN_V_DIM), fwd(LANES), gr_f,
                  bwd(GDN_QK_DIM), bwd(GDN_QK_DIM), bwd(GDN_V_DIM), bwd(LANES), gr_b, st],
        out_specs=[fwd(GDN_V_DIM), bwd(GDN_V_DIM), st],
        out_shape=[jax.ShapeDtypeStruct((b, l, GDN_V_DIM), F32),
                   jax.ShapeDtypeStruct((b, l, GDN_V_DIM), F32),
                   jax.ShapeDtypeStruct((b, N_UNITS, GDN_DK, GDN_DK), F32)],
        scratch_shapes=[pltpu.VMEM((N_UNITS, GDN_DK, GDN_DK), F32)],
        compiler_params=_params("parallel", "arbitrary"),
        name="gdn_scan",
    )(qn, kn, vv, gb, gr, qn, kn, vv, gb, gr, state0)


def _attn_kernel(q_ref, kvp_ref, kvc_ref, kvn_ref, ctx_ref, sink_ref, o_ref, *, n_blocks, n_ctx_tiles):
    i = pl.program_id(1)
    row = lax.broadcasted_iota(jnp.int32, (WINDOW, WINDOW), 0)
    col = lax.broadcasted_iota(jnp.int32, (WINDOW, WINDOW), 1)
    mask_prev = (col >= row) & (i > 0)
    mask_next = (col <= row) & (i < n_blocks - 1)
    def tiles(head, base):
        kv_head = head // (ATTN_Q_HEADS // ATTN_KV_HEADS)
        c = (base + 2 * kv_head + head % 2) * LANES
        out = [kvp_ref[0, :, c:c + LANES], kvc_ref[0, :, c:c + LANES], kvn_ref[0, :, c:c + LANES]]
        return out + [ctx_ref[0, t * WINDOW:(t + 1) * WINDOW, c:c + LANES] for t in range(n_ctx_tiles)]

    n_pairs = ATTN_Q_HEADS // 2
    left = lax.broadcasted_iota(jnp.int32, (WINDOW, LANES), 1) < HEAD_DIM

    def pair_scores(pair):
        q = q_ref[0, :, pair * LANES:(pair + 1) * LANES]
        both = [lax.dot_general(q, jnp.concatenate([ka, kb], axis=0), _NT, preferred_element_type=F32)
                for ka, kb in zip(tiles(2 * pair, 0), tiles(2 * pair + 1, 0))]
        return [t[:, :WINDOW] for t in both], [t[:, WINDOW:] for t in both]

    def softmax_terms(head, s):
        sink = sink_ref[head]
        s = [jnp.where(mask_prev, s[0], -jnp.inf), s[1], jnp.where(mask_next, s[2], -jnp.inf)] + s[3:]
        m = s[1]
        for st in s[:1] + s[2:]:
            m = jnp.maximum(m, st)
        m = jnp.maximum(jnp.max(m, axis=-1, keepdims=True), sink)
        e = [jnp.exp(st - m) for st in s]
        tot = e[0]
        for et in e[1:]:
            tot = tot + et
        inv_denom = 1.0 / (jnp.sum(tot, axis=-1, keepdims=True) + jnp.exp(sink - m))
        return [et.astype(BF16) for et in e], inv_denom

    def pair_output(pair, terms_a, terms_b):
        pv = None
        for ea, eb, va, vb in zip(terms_a[0], terms_b[0], tiles(2 * pair, 4), tiles(2 * pair + 1, 4)):
            part = jnp.dot(jnp.concatenate([ea, eb], axis=1), jnp.concatenate([va, vb], axis=0),
                           preferred_element_type=F32)
            pv = part if pv is None else pv + part
        acc = pv * jnp.where(left, terms_a[1], terms_b[1])
        o_ref[0, :, pair * LANES:(pair + 1) * LANES] = acc.astype(BF16)

    scores = [pair_scores(pair) for pair in range(n_pairs)]
    terms = [(softmax_terms(2 * pair, s[0]), softmax_terms(2 * pair + 1, s[1])) for pair, s in enumerate(scores)]
    for pair, (terms_a, terms_b) in enumerate(terms):
        pair_output(pair, terms_a, terms_b)


def _attn(q, kvm, kvm_ctx, sink):
    b, l, _ = q.shape
    lc = kvm_ctx.shape[1]
    n = l // WINDOW
    blk = lambda f: pl.BlockSpec((1, WINDOW, KVM_WIDTH), f)
    return pl.pallas_call(
        functools.partial(_attn_kernel, n_blocks=n, n_ctx_tiles=lc // WINDOW),
        grid=(b, n),
        in_specs=[pl.BlockSpec((1, WINDOW, ATTN_Q_DIM), lambda bi, i: (bi, i, 0)),
                  blk(lambda bi, i: (bi, jnp.maximum(i - 1, 0), 0)),
                  blk(lambda bi, i: (bi, i, 0)),
                  blk(lambda bi, i: (bi, jnp.minimum(i + 1, n - 1), 0)),
                  pl.BlockSpec((1, lc, KVM_WIDTH), lambda bi, i: (bi, 0, 0)),
                  pl.BlockSpec(memory_space=pltpu.SMEM)],
        out_specs=pl.BlockSpec((1, WINDOW, ATTN_Q_DIM), lambda bi, i: (bi, i, 0)),
        out_shape=jax.ShapeDtypeStruct((b, l, ATTN_Q_DIM), BF16),
        compiler_params=_params("parallel", "arbitrary"),
        name="attn",
    )(q, kvm, kvm, kvm, kvm_ctx, sink)


def _row_parts(tm):
    part = tm // FFN_PARTS
    return [slice(r * part, (r + 1) * part) for r in range(FFN_PARTS)]


def _residual_ffn(xs, ys, mod, gains, wg_ref, wu_ref, wd_ref):
    x1 = [x + mod[2:3] * (_rms(y) * gains[0:1]) for x, y in zip(xs, ys)]
    h = [_modulated_norm(t, gains[1:2], mod, 3).astype(BF16) for t in x1]
    gate_up = [(jnp.dot(t, wg_ref[...], preferred_element_type=F32),
                jnp.dot(t, wu_ref[...], preferred_element_type=F32)) for t in h]
    f = [_dot(_silu(gate) * up, wd_ref[...]) for gate, up in gate_up]
    return [t + mod[5:6] * (_rms(ft) * gains[2:3]) for t, ft in zip(x1, f)]


def _post_hy_kernel(x_ref, attn_ref, of_ref, ob_ref, z_ref, mod_ref, gains_ref, ng_ref,
                    wo_ref, wg_ref, wu_ref, wd_ref, out_ref):
    parts = _row_parts(x_ref.shape[1])
    ys = []
    for rows in parts:
        y = jnp.dot(attn_ref[0, rows, :], wo_ref[0:ATTN_Q_DIM, :], preferred_element_type=F32)
        for h in range(GDN_HEADS):
            cols = slice(h * GDN_DK, (h + 1) * GDN_DK)
            gated = (_rms(of_ref[0, rows, cols] + ob_ref[0, rows, cols]) * ng_ref[...]
                     * _silu(z_ref[0, rows, cols]))
            y = y + _dot(gated, wo_ref[ATTN_Q_DIM + h * GDN_DK:ATTN_Q_DIM + (h + 1) * GDN_DK, :])
        ys.append(y)
    outs = _residual_ffn([x_ref[0, rows, :] for rows in parts], ys, mod_ref[0], gains_ref[...],
                         wg_ref, wu_ref, wd_ref)
    for rows, out in zip(parts, outs):
        out_ref[0, rows, :] = out


def _post_hy(x, attn, o_f, o_b, z, mod, gains, norm_g, w_out, w_gate, w_up, w_down, tm):
    b, l, d = x.shape
    hidden = w_gate.shape[1]
    seq = lambda width: pl.BlockSpec((1, tm, width), lambda bi, i: (bi, i, 0))
    return pl.pallas_call(
        _post_hy_kernel,
        grid=(b, l // tm),
        in_specs=[seq(d), seq(ATTN_Q_DIM), seq(GDN_V_DIM), seq(GDN_V_DIM), seq(GDN_V_DIM),
                  pl.BlockSpec((1, 6, d), lambda bi, i: (bi, 0, 0)),
                  _resident((3, d)), _resident((1, GDN_DK)),
                  _resident((ATTN_Q_DIM + GDN_V_DIM, d)),
                  _resident((d, hidden)), _resident((d, hidden)), _resident((hidden, d))],
        out_specs=seq(d),
        out_shape=jax.ShapeDtypeStruct((b, l, d), F32),
        compiler_params=_params("parallel", "arbitrary"),
        name="post_hy",
    )(x, attn, o_f, o_b, z, mod, gains, norm_g.reshape(1, GDN_DK), w_out, w_gate, w_up, w_down)


def _inproj_sc_kernel(x_ref, mod_ref, g_ref, w_ref, b_out, cu_out):
    d = x_ref.shape[2]
    parts = _row_parts(x_ref.shape[1])
    hs = [_modulated_norm(x_ref[0, rows, :], g_ref[...], mod_ref[0], 0) for rows in parts]
    ps = [_dot(h, w_ref[...]) for h in hs]
    for rows, p in zip(parts, ps):
        b_out[0, rows, :] = p[:, 0:d]
        cu_out[0, rows, :] = p[:, d:2 * d] * p[:, 2 * d:3 * d]


def _inproj_sc(x, mod, gain, w_in, tm):
    b, l, d = x.shape
    seq = pl.BlockSpec((1, tm, d), lambda bi, i: (bi, i, 0))
    return pl.pallas_call(
        _inproj_sc_kernel,
        grid=(b, l // tm),
        in_specs=[seq, pl.BlockSpec((1, 6, d), lambda bi, i: (bi, 0, 0)),
                  _resident((1, d)), _resident((d, 3 * d))],
        out_specs=[seq, seq],
        out_shape=[jax.ShapeDtypeStruct((b, l, d), F32), jax.ShapeDtypeStruct((b, l, d), F32)],
        compiler_params=_params("parallel", "arbitrary"),
        name="inproj_sc",
    )(x, mod, gain.reshape(1, d), w_in)


def _post_sc_kernel(x_ref, b_ref, cu_ref, prev_ref, next_ref, mod_ref, gains_ref, cw_ref,
                    wo_ref, wg_ref, wu_ref, wd_ref, out_ref, *, n_tiles):
    i = pl.program_id(1)
    tm = x_ref.shape[1]
    row = lax.broadcasted_iota(jnp.int32, (tm, 1), 0)
    prev_row = jnp.where(i > 0, prev_ref[0, SUBLANES - 1:SUBLANES, :], 0.0)
    next_row = jnp.where(i < n_tiles - 1, next_ref[0, 0:1, :], 0.0)
    conv = _conv3_rows(cu_ref[0], prev_row, next_row, cw_ref[...], row)
    parts = _row_parts(tm)
    ys = [_dot(b_ref[0, rows, :] * conv[rows], wo_ref[...]) for rows in parts]
    outs = _residual_ffn([x_ref[0, rows, :] for rows in parts], ys, mod_ref[0], gains_ref[...],
                         wg_ref, wu_ref, wd_ref)
    for rows, out in zip(parts, outs):
        out_ref[0, rows, :] = out


def _post_sc(x, b_gate, cu, mod, gains, conv_w, w_out, w_gate, w_up, w_down, tm):
    b, l, d = x.shape
    hidden = w_gate.shape[1]
    n = l // tm
    seq = pl.BlockSpec((1, tm, d), lambda bi, i: (bi, i, 0))
    prev, nxt = _halo_specs(tm, d, l)
    return pl.pallas_call(
        functools.partial(_post_sc_kernel, n_tiles=n),
        grid=(b, n),
        in_specs=[seq, seq, seq, prev, nxt,
                  pl.BlockSpec((1, 6, d), lambda bi, i: (bi, 0, 0)),
                  _resident((3, d)), _resident((3, d)), _resident((d, d)),
                  _resident((d, hidden)), _resident((d, hidden)), _resident((hidden, d))],
        out_specs=seq,
        out_shape=jax.ShapeDtypeStruct((b, l, d), F32),
        compiler_params=_params("parallel", "arbitrary"),
        name="post_sc",
    )(x, b_gate, cu, cu, cu, mod, gains, conv_w, w_out, w_gate, w_up, w_down)


def _rope_tables(length):
    pos = jnp.arange(length)
    n_freq = HEAD_DIM // 4
    inv_freq = ROPE_BASE ** (-jnp.arange(n_freq, dtype=F32) / n_freq)
    ang = jnp.concatenate([(pos // GRID_W).astype(F32)[:, None] * inv_freq,
                           (pos % GRID_W).astype(F32)[:, None] * inv_freq], axis=-1)
    cos, sin = jnp.cos(ang), jnp.sin(ang)
    return jnp.tile(cos, (1, 4)), jnp.tile(jnp.concatenate([-sin, sin], axis=-1), (1, 2))


def _row_tile(length, target):
    tm = min(length, target)
    assert length % tm == 0 and tm % CHUNK == 0
    return tm


def kernel(x, c, ctx, c_ctx, ada_w, ada_b, pre_mix_g, post_mix_g, pre_ffn_g, post_ffn_g, hy_w_in, hy_w_out,
           attn_sink, gdn_conv_w, gdn_a_log, gdn_dt_bias, gdn_norm_g, sc_w_in, sc_conv_w, sc_w_out,
           ffn_w_gate, ffn_w_up, ffn_w_down):
    b, l, d = x.shape
    lc = ctx.shape[1]
    assert ada_w.shape[0] == 2 and l % CHUNK == 0 and lc % CHUNK == 0 and b + 1 <= SUBLANES
    tm = _row_tile(l, 512)
    tmc = _row_tile(lc, 256)

    cond = jnp.concatenate([c, c_ctx[None], jnp.zeros((SUBLANES - b - 1, d), F32)], axis=0)
    mod = _ada(cond, ada_w, ada_b).reshape(2, SUBLANES, 6, d)
    gains = jnp.stack([post_mix_g, pre_ffn_g, post_ffn_g], axis=1)
    bf = lambda w: w.astype(BF16)

    w_in = bf(jnp.pad(hy_w_in[0], ((0, 0), (0, HY_IN_PAD - hy_w_in.shape[2]))))
    cos, sin = _rope_tables(l)
    gdn_par = (gdn_conv_w[0], gdn_a_log[0], gdn_dt_bias[0])
    q, kvm, qn, kn, vv, z, gb, gr = _inproj_hy(x, mod[0], None, pre_mix_g[0], w_in, cos, sin, *gdn_par, tm)
    ones, zeros = jnp.ones((lc, LANES), F32), jnp.zeros((lc, LANES), F32)
    _, kvm_c, qn_c, kn_c, vv_c, _, gb_c, gr_c = _inproj_hy(ctx, mod[0], b, pre_mix_g[0], w_in, ones, zeros,
                                                          *gdn_par, tmc)

    state0 = jnp.zeros((b, N_UNITS, GDN_DK, GDN_DK), F32)
    _, _, state_c = _gdn_scan(qn_c, kn_c, vv_c, gb_c, gr_c, state0, _row_tile(lc, SCAN_ROWS))
    o_f, o_b, _ = _gdn_scan(qn, kn, vv, gb, gr, state_c, _row_tile(l, SCAN_ROWS))

    attn = _attn(q, kvm, kvm_c, attn_sink[0])
    x = _post_hy(x, attn, o_f, o_b, z, mod[0], gains[0], gdn_norm_g[0], bf(hy_w_out[0]),
                 bf(ffn_w_gate[0]), bf(ffn_w_up[0]), bf(ffn_w_down[0]), tm)

    b_gate, cu = _inproj_sc(x, mod[1], pre_mix_g[1], bf(sc_w_in[0]), tm)
    return _post_sc(x, b_gate, cu, mod[1], gains[1], sc_conv_w[0], bf(sc_w_out[0]),
                    bf(ffn_w_gate[1]), bf(ffn_w_up[1]), bf(ffn_w_down[1]), tm)
```

```python
import functools

import jax
import jax.numpy as jnp
from jax import lax
from jax.experimental import pallas as pl
from jax.experimental.pallas import tpu as pltpu

F32 = jnp.float32
BF16 = jnp.bfloat16

EPS = 1e-6
GRID_W = 64
ROPE_BASE = 10000.0
HEAD_DIM = 64
ATTN_Q_HEADS = 8
ATTN_KV_HEADS = 2
ATTN_Q_DIM = ATTN_Q_HEADS * HEAD_DIM
ATTN_KV_DIM = ATTN_KV_HEADS * HEAD_DIM
WINDOW = 128
GDN_HEADS = 4
GDN_DK = 128
GDN_QK_DIM = GDN_HEADS * GDN_DK
GDN_V_DIM = GDN_HEADS * GDN_DK
GDN_QKV_DIM = 2 * GDN_QK_DIM + GDN_V_DIM
N_UNITS = 2 * GDN_HEADS
LANES = 128
SUBLANES = 8
CHUNK = 128
SCAN_ROWS = 4 * CHUNK
LOG2_E = 1.4426950408889634
Q_SCALE = HEAD_DIM ** -0.5 * LOG2_E
ATTN_ROWS = 4 * WINDOW
FFN_PARTS = 2
COL_Q = 0
COL_K = COL_Q + ATTN_Q_DIM
COL_V = COL_K + ATTN_KV_DIM
COL_QKV = COL_V + ATTN_KV_DIM
COL_Z = COL_QKV + GDN_QKV_DIM
COL_AB = COL_Z + GDN_V_DIM
HY_IN_PAD = COL_AB + LANES
KVM_WIDTH = 8 * LANES
VMEM_LIMIT = 56 * 1024 * 1024

_NT = (((1,), (1,)), ((), ()))


def _sigmoid(x):
    return 1.0 / (1.0 + jnp.exp(-x))


def _silu(x):
    return x * _sigmoid(x)


def _rms(x):
    return x * lax.rsqrt(jnp.mean(x * x, axis=-1, keepdims=True) + EPS)


def _dot(a, b):
    return jnp.dot(a.astype(BF16), b.astype(BF16), preferred_element_type=F32)


def _params(*sem):
    return pltpu.CompilerParams(dimension_semantics=sem, vmem_limit_bytes=VMEM_LIMIT)


def _resident(shape):
    zeros = (0,) * len(shape)
    return pl.BlockSpec(shape, lambda *_: zeros, pipeline_mode=pl.Buffered(1))


def _ada_kernel(cond_ref, w_ref, b_ref, out_ref):
    s = _silu(cond_ref[...])
    out_ref[0] = jnp.dot(s, w_ref[0], precision=lax.Precision.HIGHEST,
                         preferred_element_type=F32) + b_ref[0]


def _ada(cond, ada_w, ada_b, tn=1536):
    depth, d, n = ada_w.shape
    rows = cond.shape[0]
    return pl.pallas_call(
        _ada_kernel,
        grid=(depth, n // tn),
        in_specs=[pl.BlockSpec((rows, d), lambda l, j: (0, 0)),
                  pl.BlockSpec((1, d, tn), lambda l, j: (l, 0, j)),
                  pl.BlockSpec((1, 1, tn), lambda l, j: (l, 0, j))],
        out_specs=pl.BlockSpec((1, rows, tn), lambda l, j: (l, 0, j)),
        out_shape=jax.ShapeDtypeStruct((depth, rows, n), F32),
        compiler_params=_params("arbitrary", "arbitrary"),
        name="ada",
    )(cond, ada_w, ada_b.reshape(depth, 1, n))


def _modulated_norm(x, gain, mod, shift_row):
    return _rms(x) * gain * (1.0 + mod[shift_row + 1:shift_row + 2]) + mod[shift_row:shift_row + 1]


def _inproj_hy_kernel(x_ref, xp_ref, xn_ref, mod_ref, g_ref, w_ref, cos_ref, sin_ref, cw_ref, par_ref,
                      q_out, kvm_out, qn_out, kn_out, vv_out, z_out, gb_out, gr_out, *, n_tiles):
    i = pl.program_id(1)
    tm = x_ref.shape[1]
    ext = tm + 2 * SUBLANES
    main = slice(SUBLANES, SUBLANES + tm)
    x_ext = jnp.concatenate([xp_ref[0], x_ref[0], xn_ref[0]], axis=0)
    p_ext = _dot(_modulated_norm(x_ext, g_ref[...], mod_ref[0], 0), w_ref[...])
    p = p_ext[main]

    cos = cos_ref[...]
    sin = sin_ref[...]
    lane = lax.broadcasted_iota(jnp.int32, (tm, LANES), 1)
    first_half = (lane & (HEAD_DIM - 1)) < HEAD_DIM // 2
    left = lane < HEAD_DIM

    def rope(t):
        partner = jnp.where(first_half, pltpu.roll(t, LANES - HEAD_DIM // 2, 1),
                            pltpu.roll(t, HEAD_DIM // 2, 1))
        return t * cos + partner * sin

    for j in range(ATTN_Q_DIM // LANES):
        c0 = COL_Q + j * LANES
        q_out[0, :, j * LANES:(j + 1) * LANES] = (rope(p[:, c0:c0 + LANES]) * Q_SCALE).astype(BF16)

    def masked_layouts(t):
        sw = pltpu.roll(t, HEAD_DIM, 1)
        return (jnp.where(left, t, 0.0), jnp.where(left, 0.0, sw),
                jnp.where(left, sw, 0.0), jnp.where(left, 0.0, t))

    k_lay = masked_layouts(rope(p[:, COL_K:COL_K + LANES]))
    v_lay = masked_layouts(p[:, COL_V:COL_V + LANES])
    for j, t in enumerate(k_lay + v_lay):
        kvm_out[0, :, j * LANES:(j + 1) * LANES] = t.astype(BF16)
    z_out[0] = p[:, COL_Z:COL_AB]

    has_prev = i > 0
    has_next = i < n_tiles - 1
    for j in range(GDN_QKV_DIM // LANES):
        c0 = COL_QKV + j * LANES
        cols = slice(j * LANES, (j + 1) * LANES)
        t = jnp.concatenate([jnp.where(has_prev, p_ext[0:SUBLANES, c0:c0 + LANES], 0.0),
                             p[:, c0:c0 + LANES],
                             jnp.where(has_next, p_ext[SUBLANES + tm:ext, c0:c0 + LANES], 0.0)], axis=0)
        y = (cw_ref[0:1, cols] * pltpu.roll(t, 1, 0) + cw_ref[1:2, cols] * t
             + cw_ref[2:3, cols] * pltpu.roll(t, ext - 1, 0))
        y = _silu(y[main])
        if j < GDN_HEADS:
            y = y * lax.rsqrt(jnp.sum(y * y, axis=-1, keepdims=True) + EPS) * GDN_DK ** -0.5
            qn_out[0, :, cols] = y
        elif j < 2 * GDN_HEADS:
            y = y * lax.rsqrt(jnp.sum(y * y, axis=-1, keepdims=True) + EPS)
            kn_out[0, :, j * LANES - GDN_QK_DIM:(j + 1) * LANES - GDN_QK_DIM] = y
        else:
            vv_out[0, :, j * LANES - 2 * GDN_QK_DIM:(j + 1) * LANES - 2 * GDN_QK_DIM] = y

    ab = p[:, COL_AB:HY_IN_PAD]
    t = ab + par_ref[1:2]
    softplus = jnp.maximum(t, 0.0) + jnp.log1p(jnp.exp(-jnp.abs(t)))
    g = -jnp.exp(par_ref[0:1]) * softplus
    beta = _sigmoid(ab)
    pos = lax.broadcasted_iota(jnp.int32, (tm, 1), 0) & (CHUNK - 1)
    pre = g
    suf = g
    s = 1
    while s < CHUNK:
        pre = pre + jnp.where(pos >= s, pltpu.roll(pre, s, 0), 0.0)
        suf = suf + jnp.where(pos < CHUNK - s, pltpu.roll(suf, tm - s, 0), 0.0)
        s *= 2
    gb = jnp.where(lane < GDN_HEADS, pre, jnp.where(lane < N_UNITS, suf, beta))
    gb_out[0] = gb
    gr_out[0] = gb.T[0:2 * N_UNITS]


def _inproj_hy(x, mod, mod_row, gain, w_pad, cos, sin, conv_w, a_log, dt_bias, tm):
    b, l, d = x.shape
    n = l // tm
    row_of = (lambda bi: bi) if mod_row is None else (lambda bi: mod_row)
    seq = lambda width: pl.BlockSpec((1, tm, width), lambda bi, i: (bi, i, 0))
    prev, nxt = _halo_specs(tm, d, l)
    par = jnp.zeros((SUBLANES, LANES), F32)
    par = par.at[0, :N_UNITS].set(a_log.reshape(-1)).at[1, :N_UNITS].set(dt_bias.reshape(-1))
    return pl.pallas_call(
        functools.partial(_inproj_hy_kernel, n_tiles=n),
        grid=(b, n),
        in_specs=[seq(d), prev, nxt,
                  pl.BlockSpec((1, 6, d), lambda bi, i: (row_of(bi), 0, 0)),
                  _resident((1, d)),
                  _resident((d, HY_IN_PAD)),
                  pl.BlockSpec((tm, LANES), lambda bi, i: (i, 0)),
                  pl.BlockSpec((tm, LANES), lambda bi, i: (i, 0)),
                  _resident((3, GDN_QKV_DIM)), _resident((SUBLANES, LANES))],
        out_specs=[seq(ATTN_Q_DIM), seq(KVM_WIDTH), seq(GDN_QK_DIM), seq(GDN_QK_DIM), seq(GDN_V_DIM),
                   seq(GDN_V_DIM), seq(LANES),
                   pl.BlockSpec((1, 2 * N_UNITS, tm), lambda bi, i: (bi, 0, i))],
        out_shape=[jax.ShapeDtypeStruct((b, l, ATTN_Q_DIM), BF16),
                   jax.ShapeDtypeStruct((b, l, KVM_WIDTH), BF16),
                   jax.ShapeDtypeStruct((b, l, GDN_QK_DIM), F32),
                   jax.ShapeDtypeStruct((b, l, GDN_QK_DIM), F32),
                   jax.ShapeDtypeStruct((b, l, GDN_V_DIM), F32),
                   jax.ShapeDtypeStruct((b, l, GDN_V_DIM), F32),
                   jax.ShapeDtypeStruct((b, l, LANES), F32),
                   jax.ShapeDtypeStruct((b, 2 * N_UNITS, l), F32)],
        compiler_params=_params("parallel", "arbitrary"),
        name="inproj_hy",
    )(x, x, x, mod, gain.reshape(1, d), w_pad, cos, sin, conv_w, par)


def _conv3_rows(x, prev_row, next_row, w, row):
    tm = x.shape[0]
    xp = jnp.where(row == 0, prev_row, pltpu.roll(x, 1, 0))
    xn = jnp.where(row == tm - 1, next_row, pltpu.roll(x, tm - 1, 0))
    return w[0:1] * xp + w[1:2] * x + w[2:3] * xn


def _halo_specs(tm, width, n_rows):
    per = tm // SUBLANES
    last = n_rows // SUBLANES - 1
    prev = pl.BlockSpec((1, SUBLANES, width), lambda bi, i: (bi, jnp.maximum(i * per - 1, 0), 0))
    nxt = pl.BlockSpec((1, SUBLANES, width), lambda bi, i: (bi, jnp.minimum((i + 1) * per, last), 0))
    return prev, nxt


def _unit_triangular_inverses(a_list, eye, diag_blocks, level_masks):
    ad = [jnp.where(diag_blocks, a, 0.0) for a in a_list]
    a2 = [_dot(x, x) for x in ad]
    a4 = [_dot(x, x) for x in a2]
    t = [_dot(eye - x, eye + y) for x, y in zip(ad, a2)]
    a8 = [_dot(x, x) for x in a4]
    t = [_dot(x, eye + y) for x, y in zip(t, a4)]
    t = [_dot(x, eye + y) for x, y in zip(t, a8)]
    for m in level_masks:
        at = [_dot(jnp.where(m, a, 0.0), x) for a, x in zip(a_list, t)]
        t = [x - _dot(x, y) for x, y in zip(t, at)]
    return t


def _gdn_scan_kernel(qf_ref, kf_ref, vf_ref, gbf_ref, grf_ref,
                     qb_ref, kb_ref, vb_ref, gbb_ref, grb_ref, s0_ref,
                     of_ref, ob_ref, s_out, s_scr, *, n_blocks):
    i = pl.program_id(1)

    @pl.when(i == 0)
    def _():
        s_scr[...] = s0_ref[0]

    row = lax.broadcasted_iota(jnp.int32, (CHUNK, CHUNK), 0)
    col = lax.broadcasted_iota(jnp.int32, (CHUNK, CHUNK), 1)
    eye = jnp.where(row == col, 1.0, 0.0)
    same = lambda shift: (row >> shift) == (col >> shift)
    diag_blocks = same(4)
    level_masks = [same(sh) & jnp.logical_not(same(sh - 1)) for sh in (5, 6, 7)]

    units = []
    for d, refs in enumerate(((qf_ref, kf_ref, vf_ref, gbf_ref, grf_ref, of_ref),
                              (qb_ref, kb_ref, vb_ref, gbb_ref, grb_ref, ob_ref))):
        for h in range(GDN_HEADS):
            units.append((d, h, d * GDN_HEADS + h, slice(h * GDN_DK, (h + 1) * GDN_DK)) + refs)

    n_chunks = qf_ref.shape[1] // CHUNK
    q, k, v, g_col, beta, g_last, decay, kq, a = ({} for _ in range(9))
    for c in range(n_chunks):
        rows = slice(c * CHUNK, (c + 1) * CHUNK)
        for d, h, u, cols, q_ref, k_ref, v_ref, gb_ref, gr_ref, o_ref in units:
            incl = (row >= col) if d == 0 else (row <= col)
            last = CHUNK - 1 if d == 0 else 0
            key = (c, u)
            q[key] = q_ref[0, rows, cols]
            k[key] = k_ref[0, rows, cols]
            v[key] = v_ref[0, rows, cols]
            g_col[key] = gb_ref[0, rows, u:u + 1]
            beta[key] = gb_ref[0, rows, N_UNITS + u:N_UNITS + u + 1]
            g_last[key] = g_col[key][last:last + 1, :]
            decay[key] = jnp.exp(jnp.where(incl, g_col[key] - gr_ref[0, u:u + 1, rows], -jnp.inf))
            kq[key] = lax.dot_general(jnp.concatenate([k[key], q[key]], axis=0).astype(BF16),
                                      k[key].astype(BF16), _NT, preferred_element_type=F32)
    keys = list(kq)
    for key in keys:
        strict = (row > col) if key[1] < GDN_HEADS else (row < col)
        a[key] = jnp.where(strict, beta[key] * kq[key][:CHUNK] * decay[key], 0.0)
    t_mat = dict(zip(keys, _unit_triangular_inverses([a[key] for key in keys], eye, diag_blocks, level_masks)))

    state = [s_scr[u] for u in range(N_UNITS)]
    for step in range(n_chunks):
        order = [(step if unit[0] == 0 else n_chunks - 1 - step, unit[2]) for unit in units]
        ps = [_dot(jnp.concatenate([k[key] * jnp.exp(g_col[key]), q[key] * jnp.exp(g_col[key])], axis=0),
                   state[key[1]]) for key in order]
        v_new = [_dot(t_mat[key], beta[key] * (v[key] - ps_u[:CHUNK])) for key, ps_u in zip(order, ps)]
        for unit, key, ps_u, v_new_u in zip(units, order, ps, v_new):
            cols, o_ref = unit[3], unit[9]
            o_ref[0, key[0] * CHUNK:(key[0] + 1) * CHUNK, cols] = (
                ps_u[CHUNK:] + _dot(kq[key][CHUNK:] * decay[key], v_new_u))
        for key, v_new_u in zip(order, v_new):
            k_tail = k[key] * jnp.exp(g_last[key] - g_col[key])
            state[key[1]] = state[key[1]] * jnp.exp(g_last[key]) + _dot(k_tail.T, v_new_u)
    for u in range(N_UNITS):
        s_scr[u] = state[u]

    @pl.when(i == n_blocks - 1)
    def _():
        s_out[0] = s_scr[...]


def _gdn_scan(qn, kn, vv, gb, gr, state0, rows=SCAN_ROWS):
    b, l, _ = qn.shape
    n = l // rows
    fwd = lambda width: pl.BlockSpec((1, rows, width), lambda bi, i: (bi, i, 0))
    bwd = lambda width: pl.BlockSpec((1, rows, width), lambda bi, i: (bi, n - 1 - i, 0))
    gr_f = pl.BlockSpec((1, 2 * N_UNITS, rows), lambda bi, i: (bi, 0, i))
    gr_b = pl.BlockSpec((1, 2 * N_UNITS, rows), lambda bi, i: (bi, 0, n - 1 - i))
    st = pl.BlockSpec((1, N_UNITS, GDN_DK, GDN_DK), lambda bi, i: (bi, 0, 0, 0))
    return pl.pallas_call(
        functools.partial(_gdn_scan_kernel, n_blocks=n),
        grid=(b, n),
        in_specs=[fwd(GDN_QK_DIM), fwd(GDN_QK_DIM), fwd(GDN_V_DIM), fwd(LANES), gr_f,
                  bwd(GDN_QK_DIM), bwd(GDN_QK_DIM), bwd(GDN_V_DIM), bwd(LANES), gr_b, st],
        out_specs=[fwd(GDN_V_DIM), bwd(GDN_V_DIM), st],
        out_shape=[jax.ShapeDtypeStruct((b, l, GDN_V_DIM), F32),
                   jax.ShapeDtypeStruct((b, l, GDN_V_DIM), F32),
                   jax.ShapeDtypeStruct((b, N_UNITS, GDN_DK, GDN_DK), F32)],
        scratch_shapes=[pltpu.VMEM((N_UNITS, GDN_DK, GDN_DK), F32)],
        compiler_params=_params("parallel", "arbitrary"),
        name="gdn_scan",
    )(qn, kn, vv, gb, gr, qn, kn, vv, gb, gr, state0)


def _attn_kernel(q_ref, kvp_ref, kvc_ref, kvn_ref, ctx_ref, sink_ref, o_ref, *, n_steps, n_ctx_tiles):
    i = pl.program_id(1)
    n_sub = q_ref.shape[1] // WINDOW
    row = lax.broadcasted_iota(jnp.int32, (WINDOW, WINDOW), 0)
    col = lax.broadcasted_iota(jnp.int32, (WINDOW, WINDOW), 1)
    n_pairs = ATTN_Q_HEADS // 2
    left = lax.broadcasted_iota(jnp.int32, (WINDOW, LANES), 1) < HEAD_DIM
    for sub in range(n_sub):
        _attn_block(sub, i, n_sub, n_steps, n_ctx_tiles, row, col, n_pairs, left,
                    q_ref, kvp_ref, kvc_ref, kvn_ref, ctx_ref, sink_ref, o_ref)


def _attn_block(sub, i, n_sub, n_steps, n_ctx_tiles, row, col, n_pairs, left,
                q_ref, kvp_ref, kvc_ref, kvn_ref, ctx_ref, sink_ref, o_ref):
    rows = slice(sub * WINDOW, (sub + 1) * WINDOW)
    mask_prev = (col >= row) & (i > 0) if sub == 0 else (col >= row)
    mask_next = (col <= row) & (i < n_steps - 1) if sub == n_sub - 1 else (col <= row)

    def tiles(head, base):
        kv_head = head // (ATTN_Q_HEADS // ATTN_KV_HEADS)
        c = (base + 2 * kv_head + head % 2) * LANES
        local = ([kvp_ref[0, :, c:c + LANES]]
                 + [kvc_ref[0, t * WINDOW:(t + 1) * WINDOW, c:c + LANES] for t in range(n_sub)]
                 + [kvn_ref[0, :, c:c + LANES]])
        return local[sub:sub + 3] + [ctx_ref[0, t * WINDOW:(t + 1) * WINDOW, c:c + LANES]
                                     for t in range(n_ctx_tiles)]

    def pair_scores(pair):
        q = q_ref[0, rows, pair * LANES:(pair + 1) * LANES]
        both = [lax.dot_general(q, jnp.concatenate([ka, kb], axis=0), _NT, preferred_element_type=F32)
                for ka, kb in zip(tiles(2 * pair, 0), tiles(2 * pair + 1, 0))]
        return [t[:, :WINDOW] for t in both], [t[:, WINDOW:] for t in both]

    def softmax_terms(head, s):
        sink = sink_ref[head] * LOG2_E
        s = [jnp.where(mask_prev, s[0], -jnp.inf), s[1], jnp.where(mask_next, s[2], -jnp.inf)] + s[3:]
        m = s[1]
        for st in s[:1] + s[2:]:
            m = jnp.maximum(m, st)
        m = jnp.maximum(jnp.max(m, axis=-1, keepdims=True), sink)
        e = [jnp.exp2(st - m) for st in s]
        tot = e[0]
        for et in e[1:]:
            tot = tot + et
        inv_denom = 1.0 / (jnp.sum(tot, axis=-1, keepdims=True) + jnp.exp2(sink - m))
        return [et.astype(BF16) for et in e], inv_denom

    def pair_output(pair, terms_a, terms_b):
        pv = None
        for ea, eb, va, vb in zip(terms_a[0], terms_b[0], tiles(2 * pair, 4), tiles(2 * pair + 1, 4)):
            part = jnp.dot(jnp.concatenate([ea, eb], axis=1), jnp.concatenate([va, vb], axis=0),
                           preferred_element_type=F32)
            pv = part if pv is None else pv + part
        acc = pv * jnp.where(left, terms_a[1], terms_b[1])
        o_ref[0, rows, pair * LANES:(pair + 1) * LANES] = acc.astype(BF16)

    scores = [pair_scores(pair) for pair in range(n_pairs)]
    terms = [(softmax_terms(2 * pair, s[0]), softmax_terms(2 * pair + 1, s[1])) for pair, s in enumerate(scores)]
    for pair, (terms_a, terms_b) in enumerate(terms):
        pair_output(pair, terms_a, terms_b)


def _attn(q, kvm, kvm_ctx, sink):
    b, l, _ = q.shape
    lc = kvm_ctx.shape[1]
    rows = _row_tile(l, ATTN_ROWS)
    n = l // rows
    per = rows // WINDOW
    last = l // WINDOW - 1
    blk = lambda f: pl.BlockSpec((1, WINDOW, KVM_WIDTH), f)
    return pl.pallas_call(
        functools.partial(_attn_kernel, n_steps=n, n_ctx_tiles=lc // WINDOW),
        grid=(b, n),
        in_specs=[pl.BlockSpec((1, rows, ATTN_Q_DIM), lambda bi, i: (bi, i, 0)),
                  blk(lambda bi, i: (bi, jnp.maximum(i * per - 1, 0), 0)),
                  pl.BlockSpec((1, rows, KVM_WIDTH), lambda bi, i: (bi, i, 0)),
                  blk(lambda bi, i: (bi, jnp.minimum((i + 1) * per, last), 0)),
                  pl.BlockSpec((1, lc, KVM_WIDTH), lambda bi, i: (bi, 0, 0)),
                  pl.BlockSpec(memory_space=pltpu.SMEM)],
        out_specs=pl.BlockSpec((1, rows, ATTN_Q_DIM), lambda bi, i: (bi, i, 0)),
        out_shape=jax.ShapeDtypeStruct((b, l, ATTN_Q_DIM), BF16),
        compiler_params=_params("parallel", "arbitrary"),
        name="attn",
    )(q, kvm, kvm, kvm, kvm_ctx, sink)


def _row_parts(tm):
    part = tm // FFN_PARTS
    return [slice(r * part, (r + 1) * part) for r in range(FFN_PARTS)]


def _residual_ffn(xs, ys, mod, gains, wg_ref, wu_ref, wd_ref):
    x1 = [x + mod[2:3] * (_rms(y) * gains[0:1]) for x, y in zip(xs, ys)]
    h = [_modulated_norm(t, gains[1:2], mod, 3).astype(BF16) for t in x1]
    gate_up = [(jnp.dot(t, wg_ref[...], preferred_element_type=F32),
                jnp.dot(t, wu_ref[...], preferred_element_type=F32)) for t in h]
    f = [_dot(_silu(gate) * up, wd_ref[...]) for gate, up in gate_up]
    return [t + mod[5:6] * (_rms(ft) * gains[2:3]) for t, ft in zip(x1, f)]


def _post_hy_kernel(x_ref, attn_ref, of_ref, ob_ref, z_ref, mod_ref, gains_ref, ng_ref,
                    wo_ref, wg_ref, wu_ref, wd_ref, out_ref):
    parts = _row_parts(x_ref.shape[1])
    ys = []
    for rows in parts:
        y = jnp.dot(attn_ref[0, rows, :], wo_ref[0:ATTN_Q_DIM, :], preferred_element_type=F32)
        for h in range(GDN_HEADS):
            cols = slice(h * GDN_DK, (h + 1) * GDN_DK)
            gated = (_rms(of_ref[0, rows, cols] + ob_ref[0, rows, cols]) * ng_ref[...]
                     * _silu(z_ref[0, rows, cols]))
            y = y + _dot(gated, wo_ref[ATTN_Q_DIM + h * GDN_DK:ATTN_Q_DIM + (h + 1) * GDN_DK, :])
        ys.append(y)
    outs = _residual_ffn([x_ref[0, rows, :] for rows in parts], ys, mod_ref[0], gains_ref[...],
                         wg_ref, wu_ref, wd_ref)
    for rows, out in zip(parts, outs):
        out_ref[0, rows, :] = out


def _post_hy(x, attn, o_f, o_b, z, mod, gains, norm_g, w_out, w_gate, w_up, w_down, tm):
    b, l, d = x.shape
    hidden = w_gate.shape[1]
    seq = lambda width: pl.BlockSpec((1, tm, width), lambda bi, i: (bi, i, 0))
    return pl.pallas_call(
        _post_hy_kernel,
        grid=(b, l // tm),
        in_specs=[seq(d), seq(ATTN_Q_DIM), seq(GDN_V_DIM), seq(GDN_V_DIM), seq(GDN_V_DIM),
                  pl.BlockSpec((1, 6, d), lambda bi, i: (bi, 0, 0)),
                  _resident((3, d)), _resident((1, GDN_DK)),
                  _resident((ATTN_Q_DIM + GDN_V_DIM, d)),
                  _resident((d, hidden)), _resident((d, hidden)), _resident((hidden, d))],
        out_specs=seq(d),
        out_shape=jax.ShapeDtypeStruct((b, l, d), F32),
        compiler_params=_params("parallel", "arbitrary"),
        name="post_hy",
    )(x, attn, o_f, o_b, z, mod, gains, norm_g.reshape(1, GDN_DK), w_out, w_gate, w_up, w_down)


def _inproj_sc_kernel(x_ref, mod_ref, g_ref, w_ref, b_out, cu_out):
    d = x_ref.shape[2]
    parts = _row_parts(x_ref.shape[1])
    hs = [_modulated_norm(x_ref[0, rows, :], g_ref[...], mod_ref[0], 0) for rows in parts]
    ps = [_dot(h, w_ref[...]) for h in hs]
    for rows, p in zip(parts, ps):
        b_out[0, rows, :] = p[:, 0:d]
        cu_out[0, rows, :] = p[:, d:2 * d] * p[:, 2 * d:3 * d]


def _inproj_sc(x, mod, gain, w_in, tm):
    b, l, d = x.shape
    seq = pl.BlockSpec((1, tm, d), lambda bi, i: (bi, i, 0))
    return pl.pallas_call(
        _inproj_sc_kernel,
        grid=(b, l // tm),
        in_specs=[seq, pl.BlockSpec((1, 6, d), lambda bi, i: (bi, 0, 0)),
                  _resident((1, d)), _resident((d, 3 * d))],
        out_specs=[seq, seq],
        out_shape=[jax.ShapeDtypeStruct((b, l, d), F32), jax.ShapeDtypeStruct((b, l, d), F32)],
        compiler_params=_params("parallel", "arbitrary"),
        name="inproj_sc",
    )(x, mod, gain.reshape(1, d), w_in)


def _post_sc_kernel(x_ref, b_ref, cu_ref, prev_ref, next_ref, mod_ref, gains_ref, cw_ref,
                    wo_ref, wg_ref, wu_ref, wd_ref, out_ref, *, n_tiles):
    i = pl.program_id(1)
    tm = x_ref.shape[1]
    row = lax.broadcasted_iota(jnp.int32, (tm, 1), 0)
    prev_row = jnp.where(i > 0, prev_ref[0, SUBLANES - 1:SUBLANES, :], 0.0)
    next_row = jnp.where(i < n_tiles - 1, next_ref[0, 0:1, :], 0.0)
    conv = _conv3_rows(cu_ref[0], prev_row, next_row, cw_ref[...], row)
    parts = _row_parts(tm)
    ys = [_dot(b_ref[0, rows, :] * conv[rows], wo_ref[...]) for rows in parts]
    outs = _residual_ffn([x_ref[0, rows, :] for rows in parts], ys, mod_ref[0], gains_ref[...],
                         wg_ref, wu_ref, wd_ref)
    for rows, out in zip(parts, outs):
        out_ref[0, rows, :] = out


def _post_sc(x, b_gate, cu, mod, gains, conv_w, w_out, w_gate, w_up, w_down, tm):
    b, l, d = x.shape
    hidden = w_gate.shape[1]
    n = l // tm
    seq = pl.BlockSpec((1, tm, d), lambda bi, i: (bi, i, 0))
    prev, nxt = _halo_specs(tm, d, l)
    return pl.pallas_call(
        functools.partial(_post_sc_kernel, n_tiles=n),
        grid=(b, n),
        in_specs=[seq, seq, seq, prev, nxt,
                  pl.BlockSpec((1, 6, d), lambda bi, i: (bi, 0, 0)),
                  _resident((3, d)), _resident((3, d)), _resident((d, d)),
                  _resident((d, hidden)), _resident((d, hidden)), _resident((hidden, d))],
        out_specs=seq,
        out_shape=jax.ShapeDtypeStruct((b, l, d), F32),
        compiler_params=_params("parallel", "arbitrary"),
        name="post_sc",
    )(x, b_gate, cu, cu, cu, mod, gains, conv_w, w_out, w_gate, w_up, w_down)


def _rope_tables(length):
    pos = jnp.arange(length)
    n_freq = HEAD_DIM // 4
    inv_freq = ROPE_BASE ** (-jnp.arange(n_freq, dtype=F32) / n_freq)
    ang = jnp.concatenate([(pos // GRID_W).astype(F32)[:, None] * inv_freq,
                           (pos % GRID_W).astype(F32)[:, None] * inv_freq], axis=-1)
    cos, sin = jnp.cos(ang), jnp.sin(ang)
    return jnp.tile(cos, (1, 4)), jnp.tile(jnp.concatenate([-sin, sin], axis=-1), (1, 2))


def _row_tile(length, target):
    tm = min(length, target)
    assert length % tm == 0 and tm % CHUNK == 0
    return tm


def kernel(x, c, ctx, c_ctx, ada_w, ada_b, pre_mix_g, post_mix_g, pre_ffn_g, post_ffn_g, hy_w_in, hy_w_out,
           attn_sink, gdn_conv_w, gdn_a_log, gdn_dt_bias, gdn_norm_g, sc_w_in, sc_conv_w, sc_w_out,
           ffn_w_gate, ffn_w_up, ffn_w_down):
    b, l, d = x.shape
    lc = ctx.shape[1]
    assert ada_w.shape[0] == 2 and l % CHUNK == 0 and lc % CHUNK == 0 and b + 1 <= SUBLANES
    tm = _row_tile(l, 512)
    tmc = _row_tile(lc, 256)

    cond = jnp.concatenate([c, c_ctx[None], jnp.zeros((SUBLANES - b - 1, d), F32)], axis=0)
    mod = _ada(cond, ada_w, ada_b).reshape(2, SUBLANES, 6, d)
    gains = jnp.stack([post_mix_g, pre_ffn_g, post_ffn_g], axis=1)
    bf = lambda w: w.astype(BF16)

    w_in = bf(jnp.pad(hy_w_in[0], ((0, 0), (0, HY_IN_PAD - hy_w_in.shape[2]))))
    cos, sin = _rope_tables(l)
    gdn_par = (gdn_conv_w[0], gdn_a_log[0], gdn_dt_bias[0])
    q, kvm, qn, kn, vv, z, gb, gr = _inproj_hy(x, mod[0], None, pre_mix_g[0], w_in, cos, sin, *gdn_par, tm)
    ones, zeros = jnp.ones((lc, LANES), F32), jnp.zeros((lc, LANES), F32)
    _, kvm_c, qn_c, kn_c, vv_c, _, gb_c, gr_c = _inproj_hy(ctx, mod[0], b, pre_mix_g[0], w_in, ones, zeros,
                                                          *gdn_par, tmc)

    state0 = jnp.zeros((b, N_UNITS, GDN_DK, GDN_DK), F32)
    _, _, state_c = _gdn_scan(qn_c, kn_c, vv_c, gb_c, gr_c, state0, _row_tile(lc, SCAN_ROWS))
    o_f, o_b, _ = _gdn_scan(qn, kn, vv, gb, gr, state_c, _row_tile(l, SCAN_ROWS))

    attn = _attn(q, kvm, kvm_c, attn_sink[0])
    x = _post_hy(x, attn, o_f, o_b, z, mod[0], gains[0], gdn_norm_g[0], bf(hy_w_out[0]),
                 bf(ffn_w_gate[0]), bf(ffn_w_up[0]), bf(ffn_w_down[0]), tm)

    b_gate, cu = _inproj_sc(x, mod[1], pre_mix_g[1], bf(sc_w_in[0]), tm)
    return _post_sc(x, b_gate, cu, mod[1], gains[1], sc_conv_w[0], bf(sc_w_out[0]),
                    bf(ffn_w_gate[1]), bf(ffn_w_up[1]), bf(ffn_w_down[1]), tm)
```

```python
import functools

import jax
import jax.numpy as jnp
from jax import lax
from jax.experimental import pallas as pl
from jax.experimental.pallas import tpu as pltpu

F32 = jnp.float32
BF16 = jnp.bfloat16

EPS = 1e-6
GRID_W = 64
ROPE_BASE = 10000.0
HEAD_DIM = 64
ATTN_Q_HEADS = 8
ATTN_KV_HEADS = 2
ATTN_Q_DIM = ATTN_Q_HEADS * HEAD_DIM
ATTN_KV_DIM = ATTN_KV_HEADS * HEAD_DIM
WINDOW = 128
GDN_HEADS = 4
GDN_DK = 128
GDN_QK_DIM = GDN_HEADS * GDN_DK
GDN_V_DIM = GDN_HEADS * GDN_DK
GDN_QKV_DIM = 2 * GDN_QK_DIM + GDN_V_DIM
N_UNITS = 2 * GDN_HEADS
LANES = 128
SUBLANES = 8
CHUNK = 128
SCAN_ROWS = 4 * CHUNK
LOG2_E = 1.4426950408889634
Q_SCALE = HEAD_DIM ** -0.5 * LOG2_E
ATTN_ROWS = 8 * WINDOW
FFN_PARTS = 2
COL_Q = 0
COL_K = COL_Q + ATTN_Q_DIM
COL_V = COL_K + ATTN_KV_DIM
COL_QKV = COL_V + ATTN_KV_DIM
COL_Z = COL_QKV + GDN_QKV_DIM
COL_AB = COL_Z + GDN_V_DIM
HY_IN_PAD = COL_AB + LANES
KVM_WIDTH = 8 * LANES
VMEM_LIMIT = 56 * 1024 * 1024

_NT = (((1,), (1,)), ((), ()))


def _sigmoid(x):
    return 1.0 / (1.0 + jnp.exp(-x))


def _silu(x):
    return x * _sigmoid(x)


def _rms(x):
    return x * lax.rsqrt(jnp.mean(x * x, axis=-1, keepdims=True) + EPS)


def _dot(a, b):
    return jnp.dot(a.astype(BF16), b.astype(BF16), preferred_element_type=F32)


def _params(*sem):
    return pltpu.CompilerParams(dimension_semantics=sem, vmem_limit_bytes=VMEM_LIMIT)


def _resident(shape):
    zeros = (0,) * len(shape)
    return pl.BlockSpec(shape, lambda *_: zeros, pipeline_mode=pl.Buffered(1))


def _ada_kernel(cond_ref, w_ref, b_ref, out_ref):
    s = _silu(cond_ref[...])
    out_ref[0] = jnp.dot(s, w_ref[0], precision=lax.Precision.HIGHEST,
                         preferred_element_type=F32) + b_ref[0]


def _ada(cond, ada_w, ada_b, tn=1536):
    depth, d, n = ada_w.shape
    rows = cond.shape[0]
    return pl.pallas_call(
        _ada_kernel,
        grid=(depth, n // tn),
        in_specs=[pl.BlockSpec((rows, d), lambda l, j: (0, 0)),
                  pl.BlockSpec((1, d, tn), lambda l, j: (l, 0, j)),
                  pl.BlockSpec((1, 1, tn), lambda l, j: (l, 0, j))],
        out_specs=pl.BlockSpec((1, rows, tn), lambda l, j: (l, 0, j)),
        out_shape=jax.ShapeDtypeStruct((depth, rows, n), F32),
        compiler_params=_params("arbitrary", "arbitrary"),
        name="ada",
    )(cond, ada_w, ada_b.reshape(depth, 1, n))


def _modulated_norm(x, gain, mod, shift_row):
    return _rms(x) * gain * (1.0 + mod[shift_row + 1:shift_row + 2]) + mod[shift_row:shift_row + 1]


def _inproj_hy_kernel(x_ref, xp_ref, xn_ref, mod_ref, g_ref, w_ref, cos_ref, sin_ref, cw_ref, par_ref,
                      q_out, kvm_out, qn_out, kn_out, vv_out, z_out, gb_out, gr_out, *, n_tiles):
    i = pl.program_id(1)
    n_parts = FFN_PARTS if x_ref.shape[1] % (FFN_PARTS * CHUNK) == 0 else 1
    tm = x_ref.shape[1] // n_parts
    ext = tm + 2 * SUBLANES
    x_ext = jnp.concatenate([xp_ref[0], x_ref[0], xn_ref[0]], axis=0)
    h_ext = _modulated_norm(x_ext, g_ref[...], mod_ref[0], 0)
    p_parts = [_dot(h_ext[r * tm:r * tm + ext], w_ref[...]) for r in range(n_parts)]
    for r, p_ext in enumerate(p_parts):
        _inproj_hy_finish(p_ext, slice(r * tm, (r + 1) * tm),
                          i > 0 if r == 0 else True, i < n_tiles - 1 if r == n_parts - 1 else True,
                          cos_ref, sin_ref, cw_ref, par_ref,
                          q_out, kvm_out, qn_out, kn_out, vv_out, z_out, gb_out, gr_out)


def _inproj_hy_finish(p_ext, rows, has_prev, has_next, cos_ref, sin_ref, cw_ref, par_ref,
                      q_out, kvm_out, qn_out, kn_out, vv_out, z_out, gb_out, gr_out):
    ext = p_ext.shape[0]
    tm = ext - 2 * SUBLANES
    main = slice(SUBLANES, SUBLANES + tm)
    p = p_ext[main]

    cos = cos_ref[rows, :]
    sin = sin_ref[rows, :]
    lane = lax.broadcasted_iota(jnp.int32, (tm, LANES), 1)
    first_half = (lane & (HEAD_DIM - 1)) < HEAD_DIM // 2
    left = lane < HEAD_DIM

    def rope(t):
        partner = jnp.where(first_half, pltpu.roll(t, LANES - HEAD_DIM // 2, 1),
                            pltpu.roll(t, HEAD_DIM // 2, 1))
        return t * cos + partner * sin

    for j in range(ATTN_Q_DIM // LANES):
        c0 = COL_Q + j * LANES
        q_out[0, rows, j * LANES:(j + 1) * LANES] = (rope(p[:, c0:c0 + LANES]) * Q_SCALE).astype(BF16)

    def masked_layouts(t):
        sw = pltpu.roll(t, HEAD_DIM, 1)
        return (jnp.where(left, t, 0.0), jnp.where(left, 0.0, sw),
                jnp.where(left, sw, 0.0), jnp.where(left, 0.0, t))

    k_lay = masked_layouts(rope(p[:, COL_K:COL_K + LANES]))
    v_lay = masked_layouts(p[:, COL_V:COL_V + LANES])
    for j, t in enumerate(k_lay + v_lay):
        kvm_out[0, rows, j * LANES:(j + 1) * LANES] = t.astype(BF16)
    z_out[0, rows, :] = p[:, COL_Z:COL_AB]

    for j in range(GDN_QKV_DIM // LANES):
        c0 = COL_QKV + j * LANES
        cols = slice(j * LANES, (j + 1) * LANES)
        t = jnp.concatenate([jnp.where(has_prev, p_ext[0:SUBLANES, c0:c0 + LANES], 0.0),
                             p[:, c0:c0 + LANES],
                             jnp.where(has_next, p_ext[SUBLANES + tm:ext, c0:c0 + LANES], 0.0)], axis=0)
        y = (cw_ref[0:1, cols] * pltpu.roll(t, 1, 0) + cw_ref[1:2, cols] * t
             + cw_ref[2:3, cols] * pltpu.roll(t, ext - 1, 0))
        y = _silu(y[main])
        if j < GDN_HEADS:
            y = y * lax.rsqrt(jnp.sum(y * y, axis=-1, keepdims=True) + EPS) * GDN_DK ** -0.5
            qn_out[0, rows, cols] = y
        elif j < 2 * GDN_HEADS:
            y = y * lax.rsqrt(jnp.sum(y * y, axis=-1, keepdims=True) + EPS)
            kn_out[0, rows, j * LANES - GDN_QK_DIM:(j + 1) * LANES - GDN_QK_DIM] = y
        else:
            vv_out[0, rows, j * LANES - 2 * GDN_QK_DIM:(j + 1) * LANES - 2 * GDN_QK_DIM] = y

    ab = p[:, COL_AB:HY_IN_PAD]
    t = ab + par_ref[1:2]
    softplus = jnp.maximum(t, 0.0) + jnp.log1p(jnp.exp(-jnp.abs(t)))
    g = -jnp.exp(par_ref[0:1]) * softplus
    beta = _sigmoid(ab)
    pos = lax.broadcasted_iota(jnp.int32, (tm, 1), 0) & (CHUNK - 1)
    pre = g
    suf = g
    s = 1
    while s < CHUNK:
        pre = pre + jnp.where(pos >= s, pltpu.roll(pre, s, 0), 0.0)
        suf = suf + jnp.where(pos < CHUNK - s, pltpu.roll(suf, tm - s, 0), 0.0)
        s *= 2
    gb = jnp.where(lane < GDN_HEADS, pre, jnp.where(lane < N_UNITS, suf, beta))
    gb_out[0, rows, :] = gb
    gr_out[0, :, rows] = gb.T[0:2 * N_UNITS]


def _inproj_hy(x, mod, mod_row, gain, w_pad, cos, sin, conv_w, a_log, dt_bias, tm):
    b, l, d = x.shape
    n = l // tm
    row_of = (lambda bi: bi) if mod_row is None else (lambda bi: mod_row)
    seq = lambda width: pl.BlockSpec((1, tm, width), lambda bi, i: (bi, i, 0))
    prev, nxt = _halo_specs(tm, d, l)
    par = jnp.zeros((SUBLANES, LANES), F32)
    par = par.at[0, :N_UNITS].set(a_log.reshape(-1)).at[1, :N_UNITS].set(dt_bias.reshape(-1))
    return pl.pallas_call(
        functools.partial(_inproj_hy_kernel, n_tiles=n),
        grid=(b, n),
        in_specs=[seq(d), prev, nxt,
                  pl.BlockSpec((1, 6, d), lambda bi, i: (row_of(bi), 0, 0)),
                  _resident((1, d)),
                  _resident((d, HY_IN_PAD)),
                  pl.BlockSpec((tm, LANES), lambda bi, i: (i, 0)),
                  pl.BlockSpec((tm, LANES), lambda bi, i: (i, 0)),
                  _resident((3, GDN_QKV_DIM)), _resident((SUBLANES, LANES))],
        out_specs=[seq(ATTN_Q_DIM), seq(KVM_WIDTH), seq(GDN_QK_DIM), seq(GDN_QK_DIM), seq(GDN_V_DIM),
                   seq(GDN_V_DIM), seq(LANES),
                   pl.BlockSpec((1, 2 * N_UNITS, tm), lambda bi, i: (bi, 0, i))],
        out_shape=[jax.ShapeDtypeStruct((b, l, ATTN_Q_DIM), BF16),
                   jax.ShapeDtypeStruct((b, l, KVM_WIDTH), BF16),
                   jax.ShapeDtypeStruct((b, l, GDN_QK_DIM), F32),
                   jax.ShapeDtypeStruct((b, l, GDN_QK_DIM), F32),
                   jax.ShapeDtypeStruct((b, l, GDN_V_DIM), F32),
                   jax.ShapeDtypeStruct((b, l, GDN_V_DIM), F32),
                   jax.ShapeDtypeStruct((b, l, LANES), F32),
                   jax.ShapeDtypeStruct((b, 2 * N_UNITS, l), F32)],
        compiler_params=_params("parallel", "arbitrary"),
        name="inproj_hy",
    )(x, x, x, mod, gain.reshape(1, d), w_pad, cos, sin, conv_w, par)


def _conv3_rows(x, prev_row, next_row, w, row):
    tm = x.shape[0]
    xp = jnp.where(row == 0, prev_row, pltpu.roll(x, 1, 0))
    xn = jnp.where(row == tm - 1, next_row, pltpu.roll(x, tm - 1, 0))
    return w[0:1] * xp + w[1:2] * x + w[2:3] * xn


def _halo_specs(tm, width, n_rows):
    per = tm // SUBLANES
    last = n_rows // SUBLANES - 1
    prev = pl.BlockSpec((1, SUBLANES, width), lambda bi, i: (bi, jnp.maximum(i * per - 1, 0), 0))
    nxt = pl.BlockSpec((1, SUBLANES, width), lambda bi, i: (bi, jnp.minimum((i + 1) * per, last), 0))
    return prev, nxt


def _unit_triangular_inverses(a_list, eye, diag_blocks, level_masks):
    ad = [jnp.where(diag_blocks, a, 0.0) for a in a_list]
    a2 = [_dot(x, x) for x in ad]
    a4 = [_dot(x, x) for x in a2]
    t = [_dot(eye - x, eye + y) for x, y in zip(ad, a2)]
    a8 = [_dot(x, x) for x in a4]
    t = [_dot(x, eye + y) for x, y in zip(t, a4)]
    t = [_dot(x, eye + y) for x, y in zip(t, a8)]
    for m in level_masks:
        at = [_dot(jnp.where(m, a, 0.0), x) for a, x in zip(a_list, t)]
        t = [x - _dot(x, y) for x, y in zip(t, at)]
    return t


def _gdn_scan_kernel(qf_ref, kf_ref, vf_ref, gbf_ref, grf_ref,
                     qb_ref, kb_ref, vb_ref, gbb_ref, grb_ref, s0_ref,
                     of_ref, ob_ref, s_out, s_scr, *, n_blocks):
    i = pl.program_id(1)

    @pl.when(i == 0)
    def _():
        s_scr[...] = s0_ref[0]

    row = lax.broadcasted_iota(jnp.int32, (CHUNK, CHUNK), 0)
    col = lax.broadcasted_iota(jnp.int32, (CHUNK, CHUNK), 1)
    eye = jnp.where(row == col, 1.0, 0.0)
    same = lambda shift: (row >> shift) == (col >> shift)
    diag_blocks = same(4)
    level_masks = [same(sh) & jnp.logical_not(same(sh - 1)) for sh in (5, 6, 7)]

    units = []
    for d, refs in enumerate(((qf_ref, kf_ref, vf_ref, gbf_ref, grf_ref, of_ref),
                              (qb_ref, kb_ref, vb_ref, gbb_ref, grb_ref, ob_ref))):
        for h in range(GDN_HEADS):
            units.append((d, h, d * GDN_HEADS + h, slice(h * GDN_DK, (h + 1) * GDN_DK)) + refs)

    n_chunks = qf_ref.shape[1] // CHUNK
    q, k, v, g_col, beta, g_last, decay, kq, a = ({} for _ in range(9))
    for c in range(n_chunks):
        rows = slice(c * CHUNK, (c + 1) * CHUNK)
        for d, h, u, cols, q_ref, k_ref, v_ref, gb_ref, gr_ref, o_ref in units:
            incl = (row >= col) if d == 0 else (row <= col)
            last = CHUNK - 1 if d == 0 else 0
            key = (c, u)
            q[key] = q_ref[0, rows, cols]
            k[key] = k_ref[0, rows, cols]
            v[key] = v_ref[0, rows, cols]
            g_col[key] = gb_ref[0, rows, u:u + 1]
            beta[key] = gb_ref[0, rows, N_UNITS + u:N_UNITS + u + 1]
            g_last[key] = g_col[key][last:last + 1, :]
            decay[key] = jnp.exp(jnp.where(incl, g_col[key] - gr_ref[0, u:u + 1, rows], -jnp.inf))
            kq[key] = lax.dot_general(jnp.concatenate([k[key], q[key]], axis=0).astype(BF16),
                                      k[key].astype(BF16), _NT, preferred_element_type=F32)
    keys = list(kq)
    for key in keys:
        strict = (row > col) if key[1] < GDN_HEADS else (row < col)
        a[key] = jnp.where(strict, beta[key] * kq[key][:CHUNK] * decay[key], 0.0)
    t_mat = dict(zip(keys, _unit_triangular_inverses([a[key] for key in keys], eye, diag_blocks, level_masks)))

    state = [s_scr[u] for u in range(N_UNITS)]
    for step in range(n_chunks):
        order = [(step if unit[0] == 0 else n_chunks - 1 - step, unit[2]) for unit in units]
        ps = [_dot(jnp.concatenate([k[key] * jnp.exp(g_col[key]), q[key] * jnp.exp(g_col[key])], axis=0),
                   state[key[1]]) for key in order]
        v_new = [_dot(t_mat[key], beta[key] * (v[key] - ps_u[:CHUNK])) for key, ps_u in zip(order, ps)]
        for unit, key, ps_u, v_new_u in zip(units, order, ps, v_new):
            cols, o_ref = unit[3], unit[9]
            o_ref[0, key[0] * CHUNK:(key[0] + 1) * CHUNK, cols] = (
                ps_u[CHUNK:] + _dot(kq[key][CHUNK:] * decay[key], v_new_u))
        for key, v_new_u in zip(order, v_new):
            k_tail = k[key] * jnp.exp(g_last[key] - g_col[key])
            state[key[1]] = state[key[1]] * jnp.exp(g_last[key]) + _dot(k_tail.T, v_new_u)
    for u in range(N_UNITS):
        s_scr[u] = state[u]

    @pl.when(i == n_blocks - 1)
    def _():
        s_out[0] = s_scr[...]


def _gdn_scan(qn, kn, vv, gb, gr, state0, rows=SCAN_ROWS):
    b, l, _ = qn.shape
    n = l // rows
    fwd = lambda width: pl.BlockSpec((1, rows, width), lambda bi, i: (bi, i, 0))
    bwd = lambda width: pl.BlockSpec((1, rows, width), lambda bi, i: (bi, n - 1 - i, 0))
    gr_f = pl.BlockSpec((1, 2 * N_UNITS, rows), lambda bi, i: (bi, 0, i))
    gr_b = pl.BlockSpec((1, 2 * N_UNITS, rows), lambda bi, i: (bi, 0, n - 1 - i))
    st = pl.BlockSpec((1, N_UNITS, GDN_DK, GDN_DK), lambda bi, i: (bi, 0, 0, 0))
    return pl.pallas_call(
        functools.partial(_gdn_scan_kernel, n_blocks=n),
        grid=(b, n),
        in_specs=[fwd(GDN_QK_DIM), fwd(GDN_QK_DIM), fwd(GDN_V_DIM), fwd(LANES), gr_f,
                  bwd(GDN_QK_DIM), bwd(GDN_QK_DIM), bwd(GDN_V_DIM), bwd(LANES), gr_b, st],
        out_specs=[fwd(GDN_V_DIM), bwd(GDN_V_DIM), st],
        out_shape=[jax.ShapeDtypeStruct((b, l, GDN_V_DIM), F32),
                   jax.ShapeDtypeStruct((b, l, GDN_V_DIM), F32),
                   jax.ShapeDtypeStruct((b, N_UNITS, GDN_DK, GDN_DK), F32)],
        scratch_shapes=[pltpu.VMEM((N_UNITS, GDN_DK, GDN_DK), F32)],
        compiler_params=_params("parallel", "arbitrary"),
        name="gdn_scan",
    )(qn, kn, vv, gb, gr, qn, kn, vv, gb, gr, state0)


def _attn_kernel(q_ref, kvp_ref, kvc_ref, kvn_ref, ctx_ref, sink_ref, o_ref, *, n_steps, n_ctx_tiles):
    i = pl.program_id(1)
    n_sub = q_ref.shape[1] // WINDOW
    row = lax.broadcasted_iota(jnp.int32, (WINDOW, WINDOW), 0)
    col = lax.broadcasted_iota(jnp.int32, (WINDOW, WINDOW), 1)
    n_pairs = ATTN_Q_HEADS // 2
    left = lax.broadcasted_iota(jnp.int32, (WINDOW, LANES), 1) < HEAD_DIM
    for sub in range(n_sub):
        _attn_block(sub, i, n_sub, n_steps, n_ctx_tiles, row, col, n_pairs, left,
                    q_ref, kvp_ref, kvc_ref, kvn_ref, ctx_ref, sink_ref, o_ref)


def _attn_block(sub, i, n_sub, n_steps, n_ctx_tiles, row, col, n_pairs, left,
                q_ref, kvp_ref, kvc_ref, kvn_ref, ctx_ref, sink_ref, o_ref):
    rows = slice(sub * WINDOW, (sub + 1) * WINDOW)
    mask_prev = (col >= row) & (i > 0) if sub == 0 else (col >= row)
    mask_next = (col <= row) & (i < n_steps - 1) if sub == n_sub - 1 else (col <= row)

    def tiles(head, base):
        kv_head = head // (ATTN_Q_HEADS // ATTN_KV_HEADS)
        c = (base + 2 * kv_head + head % 2) * LANES
        local = ([kvp_ref[0, :, c:c + LANES]]
                 + [kvc_ref[0, t * WINDOW:(t + 1) * WINDOW, c:c + LANES] for t in range(n_sub)]
                 + [kvn_ref[0, :, c:c + LANES]])
        return local[sub:sub + 3] + [ctx_ref[0, t * WINDOW:(t + 1) * WINDOW, c:c + LANES]
                                     for t in range(n_ctx_tiles)]

    def pair_scores(pair):
        q = q_ref[0, rows, pair * LANES:(pair + 1) * LANES]
        both = [lax.dot_general(q, jnp.concatenate([ka, kb], axis=0), _NT, preferred_element_type=F32)
                for ka, kb in zip(tiles(2 * pair, 0), tiles(2 * pair + 1, 0))]
        return [t[:, :WINDOW] for t in both], [t[:, WINDOW:] for t in both]

    def softmax_terms(head, s):
        sink = sink_ref[head] * LOG2_E
        s = [jnp.where(mask_prev, s[0], -jnp.inf), s[1], jnp.where(mask_next, s[2], -jnp.inf)] + s[3:]
        m = s[1]
        for st in s[:1] + s[2:]:
            m = jnp.maximum(m, st)
        m = jnp.maximum(jnp.max(m, axis=-1, keepdims=True), sink)
        e = [jnp.exp2(st - m) for st in s]
        tot = e[0]
        for et in e[1:]:
            tot = tot + et
        inv_denom = 1.0 / (jnp.sum(tot, axis=-1, keepdims=True) + jnp.exp2(sink - m))
        return [et.astype(BF16) for et in e], inv_denom

    def pair_output(pair, terms_a, terms_b):
        pv = None
        for ea, eb, va, vb in zip(terms_a[0], terms_b[0], tiles(2 * pair, 4), tiles(2 * pair + 1, 4)):
            part = jnp.dot(jnp.concatenate([ea, eb], axis=1), jnp.concatenate([va, vb], axis=0),
                           preferred_element_type=F32)
            pv = part if pv is None else pv + part
        acc = pv * jnp.where(left, terms_a[1], terms_b[1])
        o_ref[0, rows, pair * LANES:(pair + 1) * LANES] = acc.astype(BF16)

    scores = [pair_scores(pair) for pair in range(n_pairs)]
    terms = [(softmax_terms(2 * pair, s[0]), softmax_terms(2 * pair + 1, s[1])) for pair, s in enumerate(scores)]
    for pair, (terms_a, terms_b) in enumerate(terms):
        pair_output(pair, terms_a, terms_b)


def _attn(q, kvm, kvm_ctx, sink):
    b, l, _ = q.shape
    lc = kvm_ctx.shape[1]
    rows = _row_tile(l, ATTN_ROWS)
    n = l // rows
    per = rows // WINDOW
    last = l // WINDOW - 1
    blk = lambda f: pl.BlockSpec((1, WINDOW, KVM_WIDTH), f)
    return pl.pallas_call(
        functools.partial(_attn_kernel, n_steps=n, n_ctx_tiles=lc // WINDOW),
        grid=(b, n),
        in_specs=[pl.BlockSpec((1, rows, ATTN_Q_DIM), lambda bi, i: (bi, i, 0)),
                  blk(lambda bi, i: (bi, jnp.maximum(i * per - 1, 0), 0)),
                  pl.BlockSpec((1, rows, KVM_WIDTH), lambda bi, i: (bi, i, 0)),
                  blk(lambda bi, i: (bi, jnp.minimum((i + 1) * per, last), 0)),
                  pl.BlockSpec((1, lc, KVM_WIDTH), lambda bi, i: (bi, 0, 0)),
                  pl.BlockSpec(memory_space=pltpu.SMEM)],
        out_specs=pl.BlockSpec((1, rows, ATTN_Q_DIM), lambda bi, i: (bi, i, 0)),
        out_shape=jax.ShapeDtypeStruct((b, l, ATTN_Q_DIM), BF16),
        compiler_params=_params("parallel", "arbitrary"),
        name="attn",
    )(q, kvm, kvm, kvm, kvm_ctx, sink)


def _row_parts(tm):
    part = tm // FFN_PARTS
    return [slice(r * part, (r + 1) * part) for r in range(FFN_PARTS)]


def _residual_ffn(xs, ys, mod, gains, wg_ref, wu_ref, wd_ref):
    x1 = [x + mod[2:3] * (_rms(y) * gains[0:1]) for x, y in zip(xs, ys)]
    h = [_modulated_norm(t, gains[1:2], mod, 3).astype(BF16) for t in x1]
    gate_up = [(jnp.dot(t, wg_ref[...], preferred_element_type=F32),
                jnp.dot(t, wu_ref[...], preferred_element_type=F32)) for t in h]
    f = [_dot(_silu(gate) * up, wd_ref[...]) for gate, up in gate_up]
    return [t + mod[5:6] * (_rms(ft) * gains[2:3]) for t, ft in zip(x1, f)]


def _post_hy_kernel(x_ref, attn_ref, of_ref, ob_ref, z_ref, mod_ref, gains_ref, ng_ref,
                    wo_ref, wg_ref, wu_ref, wd_ref, out_ref):
    parts = _row_parts(x_ref.shape[1])
    ys = []
    for rows in parts:
        y = jnp.dot(attn_ref[0, rows, :], wo_ref[0:ATTN_Q_DIM, :], preferred_element_type=F32)
        for h in range(GDN_HEADS):
            cols = slice(h * GDN_DK, (h + 1) * GDN_DK)
            gated = (_rms(of_ref[0, rows, cols] + ob_ref[0, rows, cols]) * ng_ref[...]
                     * _silu(z_ref[0, rows, cols]))
            y = y + _dot(gated, wo_ref[ATTN_Q_DIM + h * GDN_DK:ATTN_Q_DIM + (h + 1) * GDN_DK, :])
        ys.append(y)
    outs = _residual_ffn([x_ref[0, rows, :] for rows in parts], ys, mod_ref[0], gains_ref[...],
                         wg_ref, wu_ref, wd_ref)
    for rows, out in zip(parts, outs):
        out_ref[0, rows, :] = out


def _post_hy(x, attn, o_f, o_b, z, mod, gains, norm_g, w_out, w_gate, w_up, w_down, tm):
    b, l, d = x.shape
    hidden = w_gate.shape[1]
    seq = lambda width: pl.BlockSpec((1, tm, width), lambda bi, i: (bi, i, 0))
    return pl.pallas_call(
        _post_hy_kernel,
        grid=(b, l // tm),
        in_specs=[seq(d), seq(ATTN_Q_DIM), seq(GDN_V_DIM), seq(GDN_V_DIM), seq(GDN_V_DIM),
                  pl.BlockSpec((1, 6, d), lambda bi, i: (bi, 0, 0)),
                  _resident((3, d)), _resident((1, GDN_DK)),
                  _resident((ATTN_Q_DIM + GDN_V_DIM, d)),
                  _resident((d, hidden)), _resident((d, hidden)), _resident((hidden, d))],
        out_specs=seq(d),
        out_shape=jax.ShapeDtypeStruct((b, l, d), F32),
        compiler_params=_params("parallel", "arbitrary"),
        name="post_hy",
    )(x, attn, o_f, o_b, z, mod, gains, norm_g.reshape(1, GDN_DK), w_out, w_gate, w_up, w_down)


def _inproj_sc_kernel(x_ref, mod_ref, g_ref, w_ref, b_out, cu_out):
    d = x_ref.shape[2]
    parts = _row_parts(x_ref.shape[1])
    hs = [_modulated_norm(x_ref[0, rows, :], g_ref[...], mod_ref[0], 0) for rows in parts]
    ps = [_dot(h, w_ref[...]) for h in hs]
    for rows, p in zip(parts, ps):
        b_out[0, rows, :] = p[:, 0:d]
        cu_out[0, rows, :] = p[:, d:2 * d] * p[:, 2 * d:3 * d]


def _inproj_sc(x, mod, gain, w_in, tm):
    b, l, d = x.shape
    seq = pl.BlockSpec((1, tm, d), lambda bi, i: (bi, i, 0))
    return pl.pallas_call(
        _inproj_sc_kernel,
        grid=(b, l // tm),
        in_specs=[seq, pl.BlockSpec((1, 6, d), lambda bi, i: (bi, 0, 0)),
                  _resident((1, d)), _resident((d, 3 * d))],
        out_specs=[seq, seq],
        out_shape=[jax.ShapeDtypeStruct((b, l, d), F32), jax.ShapeDtypeStruct((b, l, d), F32)],
        compiler_params=_params("parallel", "arbitrary"),
        name="inproj_sc",
    )(x, mod, gain.reshape(1, d), w_in)


def _post_sc_kernel(x_ref, b_ref, cu_ref, prev_ref, next_ref, mod_ref, gains_ref, cw_ref,
                    wo_ref, wg_ref, wu_ref, wd_ref, out_ref, *, n_tiles):
    i = pl.program_id(1)
    tm = x_ref.shape[1]
    row = lax.broadcasted_iota(jnp.int32, (tm, 1), 0)
    prev_row = jnp.where(i > 0, prev_ref[0, SUBLANES - 1:SUBLANES, :], 0.0)
    next_row = jnp.where(i < n_tiles - 1, next_ref[0, 0:1, :], 0.0)
    conv = _conv3_rows(cu_ref[0], prev_row, next_row, cw_ref[...], row)
    parts = _row_parts(tm)
    ys = [_dot(b_ref[0, rows, :] * conv[rows], wo_ref[...]) for rows in parts]
    outs = _residual_ffn([x_ref[0, rows, :] for rows in parts], ys, mod_ref[0], gains_ref[...],
                         wg_ref, wu_ref, wd_ref)
    for rows, out in zip(parts, outs):
        out_ref[0, rows, :] = out


def _post_sc(x, b_gate, cu, mod, gains, conv_w, w_out, w_gate, w_up, w_down, tm):
    b, l, d = x.shape
    hidden = w_gate.shape[1]
    n = l // tm
    seq = pl.BlockSpec((1, tm, d), lambda bi, i: (bi, i, 0))
    prev, nxt = _halo_specs(tm, d, l)
    return pl.pallas_call(
        functools.partial(_post_sc_kernel, n_tiles=n),
        grid=(b, n),
        in_specs=[seq, seq, seq, prev, nxt,
                  pl.BlockSpec((1, 6, d), lambda bi, i: (bi, 0, 0)),
                  _resident((3, d)), _resident((3, d)), _resident((d, d)),
                  _resident((d, hidden)), _resident((d, hidden)), _resident((hidden, d))],
        out_specs=seq,
        out_shape=jax.ShapeDtypeStruct((b, l, d), F32),
        compiler_params=_params("parallel", "arbitrary"),
        name="post_sc",
    )(x, b_gate, cu, cu, cu, mod, gains, conv_w, w_out, w_gate, w_up, w_down)


def _rope_tables(length):
    pos = jnp.arange(length)
    n_freq = HEAD_DIM // 4
    inv_freq = ROPE_BASE ** (-jnp.arange(n_freq, dtype=F32) / n_freq)
    ang = jnp.concatenate([(pos // GRID_W).astype(F32)[:, None] * inv_freq,
                           (pos % GRID_W).astype(F32)[:, None] * inv_freq], axis=-1)
    cos, sin = jnp.cos(ang), jnp.sin(ang)
    return jnp.tile(cos, (1, 4)), jnp.tile(jnp.concatenate([-sin, sin], axis=-1), (1, 2))


def _row_tile(length, target):
    tm = min(length, target)
    assert length % tm == 0 and tm % CHUNK == 0
    return tm


def kernel(x, c, ctx, c_ctx, ada_w, ada_b, pre_mix_g, post_mix_g, pre_ffn_g, post_ffn_g, hy_w_in, hy_w_out,
           attn_sink, gdn_conv_w, gdn_a_log, gdn_dt_bias, gdn_norm_g, sc_w_in, sc_conv_w, sc_w_out,
           ffn_w_gate, ffn_w_up, ffn_w_down):
    b, l, d = x.shape
    lc = ctx.shape[1]
    assert ada_w.shape[0] == 2 and l % CHUNK == 0 and lc % CHUNK == 0 and b + 1 <= SUBLANES
    tm = _row_tile(l, 512)
    tmc = _row_tile(lc, 256)

    cond = jnp.concatenate([c, c_ctx[None], jnp.zeros((SUBLANES - b - 1, d), F32)], axis=0)
    mod = _ada(cond, ada_w, ada_b).reshape(2, SUBLANES, 6, d)
    gains = jnp.stack([post_mix_g, pre_ffn_g, post_ffn_g], axis=1)
    bf = lambda w: w.astype(BF16)

    w_in = bf(jnp.pad(hy_w_in[0], ((0, 0), (0, HY_IN_PAD - hy_w_in.shape[2]))))
    cos, sin = _rope_tables(l)
    gdn_par = (gdn_conv_w[0], gdn_a_log[0], gdn_dt_bias[0])
    q, kvm, qn, kn, vv, z, gb, gr = _inproj_hy(x, mod[0], None, pre_mix_g[0], w_in, cos, sin, *gdn_par, tm)
    ones, zeros = jnp.ones((lc, LANES), F32), jnp.zeros((lc, LANES), F32)
    _, kvm_c, qn_c, kn_c, vv_c, _, gb_c, gr_c = _inproj_hy(ctx, mod[0], b, pre_mix_g[0], w_in, ones, zeros,
                                                          *gdn_par, tmc)

    state0 = jnp.zeros((b, N_UNITS, GDN_DK, GDN_DK), F32)
    _, _, state_c = _gdn_scan(qn_c, kn_c, vv_c, gb_c, gr_c, state0, _row_tile(lc, SCAN_ROWS))
    o_f, o_b, _ = _gdn_scan(qn, kn, vv, gb, gr, state_c, _row_tile(l, SCAN_ROWS))

    attn = _attn(q, kvm, kvm_c, attn_sink[0])
    x = _post_hy(x, attn, o_f, o_b, z, mod[0], gains[0], gdn_norm_g[0], bf(hy_w_out[0]),
                 bf(ffn_w_gate[0]), bf(ffn_w_up[0]), bf(ffn_w_down[0]), tm)

    b_gate, cu = _inproj_sc(x, mod[1], pre_mix_g[1], bf(sc_w_in[0]), _row_tile(l, 2 * tm))
    return _post_sc(x, b_gate, cu, mod[1], gains[1], sc_conv_w[0], bf(sc_w_out[0]),
                    bf(ffn_w_gate[1]), bf(ffn_w_up[1]), bf(ffn_w_down[1]), tm)
```

```python
import functools

import jax
import jax.numpy as jnp
from jax import lax
from jax.experimental import pallas as pl
from jax.experimental.pallas import tpu as pltpu

F32 = jnp.float32
BF16 = jnp.bfloat16

EPS = 1e-6
GRID_W = 64
ROPE_BASE = 10000.0
HEAD_DIM = 64
ATTN_Q_HEADS = 8
ATTN_KV_HEADS = 2
ATTN_Q_DIM = ATTN_Q_HEADS * HEAD_DIM
ATTN_KV_DIM = ATTN_KV_HEADS * HEAD_DIM
WINDOW = 128
GDN_HEADS = 4
GDN_DK = 128
GDN_QK_DIM = GDN_HEADS * GDN_DK
GDN_V_DIM = GDN_HEADS * GDN_DK
GDN_QKV_DIM = 2 * GDN_QK_DIM + GDN_V_DIM
N_UNITS = 2 * GDN_HEADS
LANES = 128
SUBLANES = 8
CHUNK = 128
SCAN_ROWS = 4 * CHUNK
LOG2_E = 1.4426950408889634
Q_SCALE = HEAD_DIM ** -0.5 * LOG2_E
ATTN_ROWS = 8 * WINDOW
FFN_PARTS = 2
COL_Q = 0
COL_K = COL_Q + ATTN_Q_DIM
COL_V = COL_K + ATTN_KV_DIM
COL_QKV = COL_V + ATTN_KV_DIM
COL_Z = COL_QKV + GDN_QKV_DIM
COL_AB = COL_Z + GDN_V_DIM
HY_IN_PAD = COL_AB + LANES
KVM_WIDTH = 8 * LANES
VMEM_LIMIT = 56 * 1024 * 1024

_NT = (((1,), (1,)), ((), ()))


def _sigmoid(x):
    return 1.0 / (1.0 + jnp.exp(-x))


def _silu(x):
    return x * _sigmoid(x)


def _rms(x):
    return x * lax.rsqrt(jnp.mean(x * x, axis=-1, keepdims=True) + EPS)


def _dot(a, b):
    return jnp.dot(a.astype(BF16), b.astype(BF16), preferred_element_type=F32)


def _params(*sem):
    return pltpu.CompilerParams(dimension_semantics=sem, vmem_limit_bytes=VMEM_LIMIT)


def _resident(shape):
    zeros = (0,) * len(shape)
    return pl.BlockSpec(shape, lambda *_: zeros, pipeline_mode=pl.Buffered(1))


def _ada_kernel(cond_ref, w_ref, b_ref, out_ref):
    s = _silu(cond_ref[...])
    out_ref[0] = jnp.dot(s, w_ref[0], precision=lax.Precision.HIGHEST,
                         preferred_element_type=F32) + b_ref[0]


def _ada(cond, ada_w, ada_b, tn=1536):
    depth, d, n = ada_w.shape
    rows = cond.shape[0]
    return pl.pallas_call(
        _ada_kernel,
        grid=(depth, n // tn),
        in_specs=[pl.BlockSpec((rows, d), lambda l, j: (0, 0)),
                  pl.BlockSpec((1, d, tn), lambda l, j: (l, 0, j)),
                  pl.BlockSpec((1, 1, tn), lambda l, j: (l, 0, j))],
        out_specs=pl.BlockSpec((1, rows, tn), lambda l, j: (l, 0, j)),
        out_shape=jax.ShapeDtypeStruct((depth, rows, n), F32),
        compiler_params=_params("arbitrary", "arbitrary"),
        name="ada",
    )(cond, ada_w, ada_b.reshape(depth, 1, n))


def _modulated_norm(x, gain, mod, shift_row):
    return _rms(x) * gain * (1.0 + mod[shift_row + 1:shift_row + 2]) + mod[shift_row:shift_row + 1]


def _inproj_hy_kernel(x_ref, xp_ref, xn_ref, mod_ref, g_ref, w_ref, cos_ref, sin_ref, cw_ref, par_ref,
                      q_out, kvm_out, qn_out, kn_out, vv_out, z_out, gb_out, gr_out, *, n_tiles):
    i = pl.program_id(1)
    n_parts = FFN_PARTS if x_ref.shape[1] % (FFN_PARTS * CHUNK) == 0 else 1
    tm = x_ref.shape[1] // n_parts
    ext = tm + 2 * SUBLANES
    x_ext = jnp.concatenate([xp_ref[0], x_ref[0], xn_ref[0]], axis=0)
    h_ext = _modulated_norm(x_ext, g_ref[...], mod_ref[0], 0)
    p_parts = [_dot(h_ext[r * tm:r * tm + ext], w_ref[...]) for r in range(n_parts)]
    for r, p_ext in enumerate(p_parts):
        _inproj_hy_finish(p_ext, slice(r * tm, (r + 1) * tm),
                          i > 0 if r == 0 else True, i < n_tiles - 1 if r == n_parts - 1 else True,
                          cos_ref, sin_ref, cw_ref, par_ref,
                          q_out, kvm_out, qn_out, kn_out, vv_out, z_out, gb_out, gr_out)


def _inproj_hy_finish(p_ext, rows, has_prev, has_next, cos_ref, sin_ref, cw_ref, par_ref,
                      q_out, kvm_out, qn_out, kn_out, vv_out, z_out, gb_out, gr_out):
    ext = p_ext.shape[0]
    tm = ext - 2 * SUBLANES
    main = slice(SUBLANES, SUBLANES + tm)
    p = p_ext[main]

    cos = cos_ref[rows, :]
    sin = sin_ref[rows, :]
    lane = lax.broadcasted_iota(jnp.int32, (tm, LANES), 1)
    first_half = (lane & (HEAD_DIM - 1)) < HEAD_DIM // 2
    left = lane < HEAD_DIM

    def rope(t):
        partner = jnp.where(first_half, pltpu.roll(t, LANES - HEAD_DIM // 2, 1),
                            pltpu.roll(t, HEAD_DIM // 2, 1))
        return t * cos + partner * sin

    for j in range(ATTN_Q_DIM // LANES):
        c0 = COL_Q + j * LANES
        q_out[0, rows, j * LANES:(j + 1) * LANES] = (rope(p[:, c0:c0 + LANES]) * Q_SCALE).astype(BF16)

    def masked_layouts(t):
        sw = pltpu.roll(t, HEAD_DIM, 1)
        return (jnp.where(left, t, 0.0), jnp.where(left, 0.0, sw),
                jnp.where(left, sw, 0.0), jnp.where(left, 0.0, t))

    k_lay = masked_layouts(rope(p[:, COL_K:COL_K + LANES]))
    v_lay = masked_layouts(p[:, COL_V:COL_V + LANES])
    for j, t in enumerate(k_lay + v_lay):
        kvm_out[0, rows, j * LANES:(j + 1) * LANES] = t.astype(BF16)
    z_out[0, rows, :] = p[:, COL_Z:COL_AB]

    for j in range(GDN_QKV_DIM // LANES):
        c0 = COL_QKV + j * LANES
        cols = slice(j * LANES, (j + 1) * LANES)
        t = jnp.concatenate([jnp.where(has_prev, p_ext[0:SUBLANES, c0:c0 + LANES], 0.0),
                             p[:, c0:c0 + LANES],
                             jnp.where(has_next, p_ext[SUBLANES + tm:ext, c0:c0 + LANES], 0.0)], axis=0)
        y = (cw_ref[0:1, cols] * pltpu.roll(t, 1, 0) + cw_ref[1:2, cols] * t
             + cw_ref[2:3, cols] * pltpu.roll(t, ext - 1, 0))
        y = _silu(y[main])
        if j < GDN_HEADS:
            y = y * lax.rsqrt(jnp.sum(y * y, axis=-1, keepdims=True) + EPS) * GDN_DK ** -0.5
            qn_out[0, rows, cols] = y
        elif j < 2 * GDN_HEADS:
            y = y * lax.rsqrt(jnp.sum(y * y, axis=-1, keepdims=True) + EPS)
            kn_out[0, rows, j * LANES - GDN_QK_DIM:(j + 1) * LANES - GDN_QK_DIM] = y
        else:
            vv_out[0, rows, j * LANES - 2 * GDN_QK_DIM:(j + 1) * LANES - 2 * GDN_QK_DIM] = y

    ab = p[:, COL_AB:HY_IN_PAD]
    t = ab + par_ref[1:2]
    softplus = jnp.maximum(t, 0.0) + jnp.log1p(jnp.exp(-jnp.abs(t)))
    g = -jnp.exp(par_ref[0:1]) * softplus
    beta = _sigmoid(ab)
    pos = lax.broadcasted_iota(jnp.int32, (tm, 1), 0) & (CHUNK - 1)
    pre = g
    suf = g
    s = 1
    while s < CHUNK:
        pre = pre + jnp.where(pos >= s, pltpu.roll(pre, s, 0), 0.0)
        suf = suf + jnp.where(pos < CHUNK - s, pltpu.roll(suf, tm - s, 0), 0.0)
        s *= 2
    gb = jnp.where(lane < GDN_HEADS, pre, jnp.where(lane < N_UNITS, suf, beta))
    gb_out[0, rows, :] = gb
    gr_out[0, :, rows] = gb.T[0:2 * N_UNITS]


def _inproj_hy(x, mod, mod_row, gain, w_pad, cos, sin, conv_w, a_log, dt_bias, tm):
    b, l, d = x.shape
    n = l // tm
    row_of = (lambda bi: bi) if mod_row is None else (lambda bi: mod_row)
    seq = lambda width: pl.BlockSpec((1, tm, width), lambda bi, i: (bi, i, 0))
    prev, nxt = _halo_specs(tm, d, l)
    par = jnp.zeros((SUBLANES, LANES), F32)
    par = par.at[0, :N_UNITS].set(a_log.reshape(-1)).at[1, :N_UNITS].set(dt_bias.reshape(-1))
    return pl.pallas_call(
        functools.partial(_inproj_hy_kernel, n_tiles=n),
        grid=(b, n),
        in_specs=[seq(d), prev, nxt,
                  pl.BlockSpec((1, 6, d), lambda bi, i: (row_of(bi), 0, 0)),
                  _resident((1, d)),
                  _resident((d, HY_IN_PAD)),
                  pl.BlockSpec((tm, LANES), lambda bi, i: (i, 0)),
                  pl.BlockSpec((tm, LANES), lambda bi, i: (i, 0)),
                  _resident((3, GDN_QKV_DIM)), _resident((SUBLANES, LANES))],
        out_specs=[seq(ATTN_Q_DIM), seq(KVM_WIDTH), seq(GDN_QK_DIM), seq(GDN_QK_DIM), seq(GDN_V_DIM),
                   seq(GDN_V_DIM), seq(LANES),
                   pl.BlockSpec((1, 2 * N_UNITS, tm), lambda bi, i: (bi, 0, i))],
        out_shape=[jax.ShapeDtypeStruct((b, l, ATTN_Q_DIM), BF16),
                   jax.ShapeDtypeStruct((b, l, KVM_WIDTH), BF16),
                   jax.ShapeDtypeStruct((b, l, GDN_QK_DIM), F32),
                   jax.ShapeDtypeStruct((b, l, GDN_QK_DIM), F32),
                   jax.ShapeDtypeStruct((b, l, GDN_V_DIM), F32),
                   jax.ShapeDtypeStruct((b, l, GDN_V_DIM), F32),
                   jax.ShapeDtypeStruct((b, l, LANES), F32),
                   jax.ShapeDtypeStruct((b, 2 * N_UNITS, l), F32)],
        compiler_params=_params("parallel", "arbitrary"),
        name="inproj_hy",
    )(x, x, x, mod, gain.reshape(1, d), w_pad, cos, sin, conv_w, par)


def _conv3_rows(x, prev_row, next_row, w, row):
    tm = x.shape[0]
    xp = jnp.where(row == 0, prev_row, pltpu.roll(x, 1, 0))
    xn = jnp.where(row == tm - 1, next_row, pltpu.roll(x, tm - 1, 0))
    return w[0:1] * xp + w[1:2] * x + w[2:3] * xn


def _halo_specs(tm, width, n_rows):
    per = tm // SUBLANES
    last = n_rows // SUBLANES - 1
    prev = pl.BlockSpec((1, SUBLANES, width), lambda bi, i: (bi, jnp.maximum(i * per - 1, 0), 0))
    nxt = pl.BlockSpec((1, SUBLANES, width), lambda bi, i: (bi, jnp.minimum((i + 1) * per, last), 0))
    return prev, nxt


LEVEL_SIZES = (32, 64, 128)


def _unit_triangular_inverses(a_list, lower, eye, diag_blocks, level_masks):
    ad = [jnp.where(diag_blocks, a, 0.0) for a in a_list]
    a2 = [_dot(x, x) for x in ad]
    a4 = [_dot(x, x) for x in a2]
    t = [_dot(eye - x, eye + y) for x, y in zip(ad, a2)]
    a8 = [_dot(x, x) for x in a4]
    t = [_dot(x, eye + y) for x, y in zip(t, a4)]
    t = [_dot(x, eye + y) for x, y in zip(t, a8)]
    for m, size in zip(level_masks, LEVEL_SIZES):
        half = size // 2
        starts = [[blk * size + (half if low else 0) for blk in range(CHUNK // size)] for low in lower]
        picked = [jnp.concatenate([x[st:st + half] for st in sts], axis=0) for x, sts in zip(t, starts)]
        ta = [_dot(p, jnp.where(m, a, 0.0)) for p, a in zip(picked, a_list)]
        new = [p - _dot(y, x) for p, y, x in zip(picked, ta, t)]
        merged = []
        for x, rows, sts, low in zip(t, new, starts, lower):
            pieces = []
            for n, st in enumerate(sts):
                fresh = rows[n * half:(n + 1) * half]
                pieces += [x[st - half:st], fresh] if low else [fresh, x[st + half:st + size]]
            merged.append(jnp.concatenate(pieces, axis=0))
        t = merged
    return t


def _gdn_scan_kernel(qf_ref, kf_ref, vf_ref, gbf_ref, grf_ref,
                     qb_ref, kb_ref, vb_ref, gbb_ref, grb_ref, s0_ref,
                     of_ref, ob_ref, s_out, s_scr, *, n_blocks):
    i = pl.program_id(1)

    @pl.when(i == 0)
    def _():
        s_scr[...] = s0_ref[0]

    row = lax.broadcasted_iota(jnp.int32, (CHUNK, CHUNK), 0)
    col = lax.broadcasted_iota(jnp.int32, (CHUNK, CHUNK), 1)
    eye = jnp.where(row == col, 1.0, 0.0)
    same = lambda shift: (row >> shift) == (col >> shift)
    diag_blocks = same(4)
    level_masks = [same(sh) & jnp.logical_not(same(sh - 1)) for sh in (5, 6, 7)]

    units = []
    for d, refs in enumerate(((qf_ref, kf_ref, vf_ref, gbf_ref, grf_ref, of_ref),
                              (qb_ref, kb_ref, vb_ref, gbb_ref, grb_ref, ob_ref))):
        for h in range(GDN_HEADS):
            units.append((d, h, d * GDN_HEADS + h, slice(h * GDN_DK, (h + 1) * GDN_DK)) + refs)

    n_chunks = qf_ref.shape[1] // CHUNK
    q, k, v, g_col, beta, g_last, decay, kq, a = ({} for _ in range(9))
    for c in range(n_chunks):
        rows = slice(c * CHUNK, (c + 1) * CHUNK)
        for d, h, u, cols, q_ref, k_ref, v_ref, gb_ref, gr_ref, o_ref in units:
            incl = (row >= col) if d == 0 else (row <= col)
            last = CHUNK - 1 if d == 0 else 0
            key = (c, u)
            q[key] = q_ref[0, rows, cols]
            k[key] = k_ref[0, rows, cols]
            v[key] = v_ref[0, rows, cols]
            g_col[key] = gb_ref[0, rows, u:u + 1]
            beta[key] = gb_ref[0, rows, N_UNITS + u:N_UNITS + u + 1]
            g_last[key] = g_col[key][last:last + 1, :]
            decay[key] = jnp.exp(jnp.where(incl, g_col[key] - gr_ref[0, u:u + 1, rows], -jnp.inf))
            kq[key] = lax.dot_general(jnp.concatenate([k[key], q[key]], axis=0).astype(BF16),
                                      k[key].astype(BF16), _NT, preferred_element_type=F32)
    keys = list(kq)
    for key in keys:
        strict = (row > col) if key[1] < GDN_HEADS else (row < col)
        a[key] = jnp.where(strict, beta[key] * kq[key][:CHUNK] * decay[key], 0.0)
    t_mat = dict(zip(keys, _unit_triangular_inverses([a[key] for key in keys],
                                                     [key[1] < GDN_HEADS for key in keys],
                                                     eye, diag_blocks, level_masks)))

    state = [s_scr[u] for u in range(N_UNITS)]
    for step in range(n_chunks):
        order = [(step if unit[0] == 0 else n_chunks - 1 - step, unit[2]) for unit in units]
        ps = [_dot(jnp.concatenate([k[key] * jnp.exp(g_col[key]), q[key] * jnp.exp(g_col[key])], axis=0),
                   state[key[1]]) for key in order]
        v_new = [_dot(t_mat[key], beta[key] * (v[key] - ps_u[:CHUNK])) for key, ps_u in zip(order, ps)]
        for unit, key, ps_u, v_new_u in zip(units, order, ps, v_new):
            cols, o_ref = unit[3], unit[9]
            o_ref[0, key[0] * CHUNK:(key[0] + 1) * CHUNK, cols] = (
                ps_u[CHUNK:] + _dot(kq[key][CHUNK:] * decay[key], v_new_u))
        for key, v_new_u in zip(order, v_new):
            k_tail = k[key] * jnp.exp(g_last[key] - g_col[key])
            state[key[1]] = state[key[1]] * jnp.exp(g_last[key]) + _dot(k_tail.T, v_new_u)
    for u in range(N_UNITS):
        s_scr[u] = state[u]

    @pl.when(i == n_blocks - 1)
    def _():
        s_out[0] = s_scr[...]


def _gdn_scan(qn, kn, vv, gb, gr, state0, rows=SCAN_ROWS):
    b, l, _ = qn.shape
    n = l // rows
    fwd = lambda width: pl.BlockSpec((1, rows, width), lambda bi, i: (bi, i, 0))
    bwd = lambda width: pl.BlockSpec((1, rows, width), lambda bi, i: (bi, n - 1 - i, 0))
    gr_f = pl.BlockSpec((1, 2 * N_UNITS, rows), lambda bi, i: (bi, 0, i))
    gr_b = pl.BlockSpec((1, 2 * N_UNITS, rows), lambda bi, i: (bi, 0, n - 1 - i))
    st = pl.BlockSpec((1, N_UNITS, GDN_DK, GDN_DK), lambda bi, i: (bi, 0, 0, 0))
    return pl.pallas_call(
        functools.partial(_gdn_scan_kernel, n_blocks=n),
        grid=(b, n),
        in_specs=[fwd(GDN_QK_DIM), fwd(GDN_QK_DIM), fwd(GDN_V_DIM), fwd(LANES), gr_f,
                  bwd(GDN_QK_DIM), bwd(GDN_QK_DIM), bwd(GDN_V_DIM), bwd(LANES), gr_b, st],
        out_specs=[fwd(GDN_V_DIM), bwd(GDN_V_DIM), st],
        out_shape=[jax.ShapeDtypeStruct((b, l, GDN_V_DIM), F32),
                   jax.ShapeDtypeStruct((b, l, GDN_V_DIM), F32),
                   jax.ShapeDtypeStruct((b, N_UNITS, GDN_DK, GDN_DK), F32)],
        scratch_shapes=[pltpu.VMEM((N_UNITS, GDN_DK, GDN_DK), F32)],
        compiler_params=_params("parallel", "arbitrary"),
        name="gdn_scan",
    )(qn, kn, vv, gb, gr, qn, kn, vv, gb, gr, state0)


def _attn_kernel(q_ref, kvp_ref, kvc_ref, kvn_ref, ctx_ref, sink_ref, o_ref, *, n_steps, n_ctx_tiles):
    i = pl.program_id(1)
    n_sub = q_ref.shape[1] // WINDOW
    row = lax.broadcasted_iota(jnp.int32, (WINDOW, WINDOW), 0)
    col = lax.broadcasted_iota(jnp.int32, (WINDOW, WINDOW), 1)
    n_pairs = ATTN_Q_HEADS // 2
    left = lax.broadcasted_iota(jnp.int32, (WINDOW, LANES), 1) < HEAD_DIM
    for sub in range(n_sub):
        _attn_block(sub, i, n_sub, n_steps, n_ctx_tiles, row, col, n_pairs, left,
                    q_ref, kvp_ref, kvc_ref, kvn_ref, ctx_ref, sink_ref, o_ref)


def _attn_block(sub, i, n_sub, n_steps, n_ctx_tiles, row, col, n_pairs, left,
                q_ref, kvp_ref, kvc_ref, kvn_ref, ctx_ref, sink_ref, o_ref):
    rows = slice(sub * WINDOW, (sub + 1) * WINDOW)
    mask_prev = (col >= row) & (i > 0) if sub == 0 else (col >= row)
    mask_next = (col <= row) & (i < n_steps - 1) if sub == n_sub - 1 else (col <= row)

    def tiles(head, base):
        kv_head = head // (ATTN_Q_HEADS // ATTN_KV_HEADS)
        c = (base + 2 * kv_head + head % 2) * LANES
        local = ([kvp_ref[0, :, c:c + LANES]]
                 + [kvc_ref[0, t * WINDOW:(t + 1) * WINDOW, c:c + LANES] for t in range(n_sub)]
                 + [kvn_ref[0, :, c:c + LANES]])
        return local[sub:sub + 3] + [ctx_ref[0, t * WINDOW:(t + 1) * WINDOW, c:c + LANES]
                                     for t in range(n_ctx_tiles)]

    def pair_scores(pair):
        q = q_ref[0, rows, pair * LANES:(pair + 1) * LANES]
        both = [lax.dot_general(q, jnp.concatenate([ka, kb], axis=0), _NT, preferred_element_type=F32)
                for ka, kb in zip(tiles(2 * pair, 0), tiles(2 * pair + 1, 0))]
        return [t[:, :WINDOW] for t in both], [t[:, WINDOW:] for t in both]

    def softmax_terms(head, s):
        sink = sink_ref[head] * LOG2_E
        s = [jnp.where(mask_prev, s[0], -jnp.inf), s[1], jnp.where(mask_next, s[2], -jnp.inf)] + s[3:]
        m = s[1]
        for st in s[:1] + s[2:]:
            m = jnp.maximum(m, st)
        m = jnp.maximum(jnp.max(m, axis=-1, keepdims=True), sink)
        e = [jnp.exp2(st - m) for st in s]
        tot = e[0]
        for et in e[1:]:
            tot = tot + et
        inv_denom = 1.0 / (jnp.sum(tot, axis=-1, keepdims=True) + jnp.exp2(sink - m))
        return [et.astype(BF16) for et in e], inv_denom

    def pair_output(pair, terms_a, terms_b):
        pv = None
        for ea, eb, va, vb in zip(terms_a[0], terms_b[0], tiles(2 * pair, 4), tiles(2 * pair + 1, 4)):
            part = jnp.dot(jnp.concatenate([ea, eb], axis=1), jnp.concatenate([va, vb], axis=0),
                           preferred_element_type=F32)
            pv = part if pv is None else pv + part
        acc = pv * jnp.where(left, terms_a[1], terms_b[1])
        o_ref[0, rows, pair * LANES:(pair + 1) * LANES] = acc.astype(BF16)

    scores = [pair_scores(pair) for pair in range(n_pairs)]
    terms = [(softmax_terms(2 * pair, s[0]), softmax_terms(2 * pair + 1, s[1])) for pair, s in enumerate(scores)]
    for pair, (terms_a, terms_b) in enumerate(terms):
        pair_output(pair, terms_a, terms_b)


def _attn(q, kvm, kvm_ctx, sink):
    b, l, _ = q.shape
    lc = kvm_ctx.shape[1]
    rows = _row_tile(l, ATTN_ROWS)
    n = l // rows
    per = rows // WINDOW
    last = l // WINDOW - 1
    blk = lambda f: pl.BlockSpec((1, WINDOW, KVM_WIDTH), f)
    return pl.pallas_call(
        functools.partial(_attn_kernel, n_steps=n, n_ctx_tiles=lc // WINDOW),
        grid=(b, n),
        in_specs=[pl.BlockSpec((1, rows, ATTN_Q_DIM), lambda bi, i: (bi, i, 0)),
                  blk(lambda bi, i: (bi, jnp.maximum(i * per - 1, 0), 0)),
                  pl.BlockSpec((1, rows, KVM_WIDTH), lambda bi, i: (bi, i, 0)),
                  blk(lambda bi, i: (bi, jnp.minimum((i + 1) * per, last), 0)),
                  pl.BlockSpec((1, lc, KVM_WIDTH), lambda bi, i: (bi, 0, 0)),
                  pl.BlockSpec(memory_space=pltpu.SMEM)],
        out_specs=pl.BlockSpec((1, rows, ATTN_Q_DIM), lambda bi, i: (bi, i, 0)),
        out_shape=jax.ShapeDtypeStruct((b, l, ATTN_Q_DIM), BF16),
        compiler_params=_params("parallel", "arbitrary"),
        name="attn",
    )(q, kvm, kvm, kvm, kvm_ctx, sink)


def _row_parts(tm):
    part = tm // FFN_PARTS
    return [slice(r * part, (r + 1) * part) for r in range(FFN_PARTS)]


def _residual_ffn(xs, ys, mod, gains, wg_ref, wu_ref, wd_ref):
    x1 = [x + mod[2:3] * (_rms(y) * gains[0:1]) for x, y in zip(xs, ys)]
    h = [_modulated_norm(t, gains[1:2], mod, 3).astype(BF16) for t in x1]
    gate_up = [(jnp.dot(t, wg_ref[...], preferred_element_type=F32),
                jnp.dot(t, wu_ref[...], preferred_element_type=F32)) for t in h]
    f = [_dot(_silu(gate) * up, wd_ref[...]) for gate, up in gate_up]
    return [t + mod[5:6] * (_rms(ft) * gains[2:3]) for t, ft in zip(x1, f)]


def _post_hy_kernel(x_ref, attn_ref, of_ref, ob_ref, z_ref, mod_ref, gains_ref, ng_ref,
                    wo_ref, wg_ref, wu_ref, wd_ref, out_ref):
    parts = _row_parts(x_ref.shape[1])
    ys = []
    for rows in parts:
        y = jnp.dot(attn_ref[0, rows, :], wo_ref[0:ATTN_Q_DIM, :], preferred_element_type=F32)
        for h in range(GDN_HEADS):
            cols = slice(h * GDN_DK, (h + 1) * GDN_DK)
            gated = (_rms(of_ref[0, rows, cols] + ob_ref[0, rows, cols]) * ng_ref[...]
                     * _silu(z_ref[0, rows, cols]))
            y = y + _dot(gated, wo_ref[ATTN_Q_DIM + h * GDN_DK:ATTN_Q_DIM + (h + 1) * GDN_DK, :])
        ys.append(y)
    outs = _residual_ffn([x_ref[0, rows, :] for rows in parts], ys, mod_ref[0], gains_ref[...],
                         wg_ref, wu_ref, wd_ref)
    for rows, out in zip(parts, outs):
        out_ref[0, rows, :] = out


def _post_hy(x, attn, o_f, o_b, z, mod, gains, norm_g, w_out, w_gate, w_up, w_down, tm):
    b, l, d = x.shape
    hidden = w_gate.shape[1]
    seq = lambda width: pl.BlockSpec((1, tm, width), lambda bi, i: (bi, i, 0))
    return pl.pallas_call(
        _post_hy_kernel,
        grid=(b, l // tm),
        in_specs=[seq(d), seq(ATTN_Q_DIM), seq(GDN_V_DIM), seq(GDN_V_DIM), seq(GDN_V_DIM),
                  pl.BlockSpec((1, 6, d), lambda bi, i: (bi, 0, 0)),
                  _resident((3, d)), _resident((1, GDN_DK)),
                  _resident((ATTN_Q_DIM + GDN_V_DIM, d)),
                  _resident((d, hidden)), _resident((d, hidden)), _resident((hidden, d))],
        out_specs=seq(d),
        out_shape=jax.ShapeDtypeStruct((b, l, d), F32),
        compiler_params=_params("parallel", "arbitrary"),
        name="post_hy",
    )(x, attn, o_f, o_b, z, mod, gains, norm_g.reshape(1, GDN_DK), w_out, w_gate, w_up, w_down)


def _inproj_sc_kernel(x_ref, mod_ref, g_ref, w_ref, b_out, cu_out):
    d = x_ref.shape[2]
    parts = _row_parts(x_ref.shape[1])
    hs = [_modulated_norm(x_ref[0, rows, :], g_ref[...], mod_ref[0], 0) for rows in parts]
    ps = [_dot(h, w_ref[...]) for h in hs]
    for rows, p in zip(parts, ps):
        b_out[0, rows, :] = p[:, 0:d]
        cu_out[0, rows, :] = p[:, d:2 * d] * p[:, 2 * d:3 * d]


def _inproj_sc(x, mod, gain, w_in, tm):
    b, l, d = x.shape
    seq = pl.BlockSpec((1, tm, d), lambda bi, i: (bi, i, 0))
    return pl.pallas_call(
        _inproj_sc_kernel,
        grid=(b, l // tm),
        in_specs=[seq, pl.BlockSpec((1, 6, d), lambda bi, i: (bi, 0, 0)),
                  _resident((1, d)), _resident((d, 3 * d))],
        out_specs=[seq, seq],
        out_shape=[jax.ShapeDtypeStruct((b, l, d), F32), jax.ShapeDtypeStruct((b, l, d), F32)],
        compiler_params=_params("parallel", "arbitrary"),
        name="inproj_sc",
    )(x, mod, gain.reshape(1, d), w_in)


def _post_sc_kernel(x_ref, b_ref, cu_ref, prev_ref, next_ref, mod_ref, gains_ref, cw_ref,
                    wo_ref, wg_ref, wu_ref, wd_ref, out_ref, *, n_tiles):
    i = pl.program_id(1)
    tm = x_ref.shape[1]
    row = lax.broadcasted_iota(jnp.int32, (tm, 1), 0)
    prev_row = jnp.where(i > 0, prev_ref[0, SUBLANES - 1:SUBLANES, :], 0.0)
    next_row = jnp.where(i < n_tiles - 1, next_ref[0, 0:1, :], 0.0)
    conv = _conv3_rows(cu_ref[0], prev_row, next_row, cw_ref[...], row)
    parts = _row_parts(tm)
    ys = [_dot(b_ref[0, rows, :] * conv[rows], wo_ref[...]) for rows in parts]
    outs = _residual_ffn([x_ref[0, rows, :] for rows in parts], ys, mod_ref[0], gains_ref[...],
                         wg_ref, wu_ref, wd_ref)
    for rows, out in zip(parts, outs):
        out_ref[0, rows, :] = out


def _post_sc(x, b_gate, cu, mod, gains, conv_w, w_out, w_gate, w_up, w_down, tm):
    b, l, d = x.shape
    hidden = w_gate.shape[1]
    n = l // tm
    seq = pl.BlockSpec((1, tm, d), lambda bi, i: (bi, i, 0))
    prev, nxt = _halo_specs(tm, d, l)
    return pl.pallas_call(
        functools.partial(_post_sc_kernel, n_tiles=n),
        grid=(b, n),
        in_specs=[seq, seq, seq, prev, nxt,
                  pl.BlockSpec((1, 6, d), lambda bi, i: (bi, 0, 0)),
                  _resident((3, d)), _resident((3, d)), _resident((d, d)),
                  _resident((d, hidden)), _resident((d, hidden)), _resident((hidden, d))],
        out_specs=seq,
        out_shape=jax.ShapeDtypeStruct((b, l, d), F32),
        compiler_params=_params("parallel", "arbitrary"),
        name="post_sc",
    )(x, b_gate, cu, cu, cu, mod, gains, conv_w, w_out, w_gate, w_up, w_down)


def _rope_tables(length):
    pos = jnp.arange(length)
    n_freq = HEAD_DIM // 4
    inv_freq = ROPE_BASE ** (-jnp.arange(n_freq, dtype=F32) / n_freq)
    ang = jnp.concatenate([(pos // GRID_W).astype(F32)[:, None] * inv_freq,
                           (pos % GRID_W).astype(F32)[:, None] * inv_freq], axis=-1)
    cos, sin = jnp.cos(ang), jnp.sin(ang)
    return jnp.tile(cos, (1, 4)), jnp.tile(jnp.concatenate([-sin, sin], axis=-1), (1, 2))


def _row_tile(length, target):
    tm = min(length, target)
    assert length % tm == 0 and tm % CHUNK == 0
    return tm


def kernel(x, c, ctx, c_ctx, ada_w, ada_b, pre_mix_g, post_mix_g, pre_ffn_g, post_ffn_g, hy_w_in, hy_w_out,
           attn_sink, gdn_conv_w, gdn_a_log, gdn_dt_bias, gdn_norm_g, sc_w_in, sc_conv_w, sc_w_out,
           ffn_w_gate, ffn_w_up, ffn_w_down):
    b, l, d = x.shape
    lc = ctx.shape[1]
    assert ada_w.shape[0] == 2 and l % CHUNK == 0 and lc % CHUNK == 0 and b + 1 <= SUBLANES
    tm = _row_tile(l, 512)
    tmc = _row_tile(lc, 256)

    cond = jnp.concatenate([c, c_ctx[None], jnp.zeros((SUBLANES - b - 1, d), F32)], axis=0)
    mod = _ada(cond, ada_w, ada_b).reshape(2, SUBLANES, 6, d)
    gains = jnp.stack([post_mix_g, pre_ffn_g, post_ffn_g], axis=1)
    bf = lambda w: w.astype(BF16)

    w_in = bf(jnp.pad(hy_w_in[0], ((0, 0), (0, HY_IN_PAD - hy_w_in.shape[2]))))
    cos, sin = _rope_tables(l)
    gdn_par = (gdn_conv_w[0], gdn_a_log[0], gdn_dt_bias[0])
    q, kvm, qn, kn, vv, z, gb, gr = _inproj_hy(x, mod[0], None, pre_mix_g[0], w_in, cos, sin, *gdn_par, tm)
    ones, zeros = jnp.ones((lc, LANES), F32), jnp.zeros((lc, LANES), F32)
    _, kvm_c, qn_c, kn_c, vv_c, _, gb_c, gr_c = _inproj_hy(ctx, mod[0], b, pre_mix_g[0], w_in, ones, zeros,
                                                          *gdn_par, tmc)

    state0 = jnp.zeros((b, N_UNITS, GDN_DK, GDN_DK), F32)
    _, _, state_c = _gdn_scan(qn_c, kn_c, vv_c, gb_c, gr_c, state0, _row_tile(lc, SCAN_ROWS))
    o_f, o_b, _ = _gdn_scan(qn, kn, vv, gb, gr, state_c, _row_tile(l, SCAN_ROWS))

    attn = _attn(q, kvm, kvm_c, attn_sink[0])
    x = _post_hy(x, attn, o_f, o_b, z, mod[0], gains[0], gdn_norm_g[0], bf(hy_w_out[0]),
                 bf(ffn_w_gate[0]), bf(ffn_w_up[0]), bf(ffn_w_down[0]), tm)

    b_gate, cu = _inproj_sc(x, mod[1], pre_mix_g[1], bf(sc_w_in[0]), _row_tile(l, 2 * tm))
    return _post_sc(x, b_gate, cu, mod[1], gains[1], sc_conv_w[0], bf(sc_w_out[0]),
                    bf(ffn_w_gate[1]), bf(ffn_w_up[1]), bf(ffn_w_down[1]), tm)
```

```python
import functools

import jax
import jax.numpy as jnp
from jax import lax
from jax.experimental import pallas as pl
from jax.experimental.pallas import tpu as pltpu

F32 = jnp.float32
BF16 = jnp.bfloat16

EPS = 1e-6
GRID_W = 64
ROPE_BASE = 10000.0
HEAD_DIM = 64
ATTN_Q_HEADS = 8
ATTN_KV_HEADS = 2
ATTN_Q_DIM = ATTN_Q_HEADS * HEAD_DIM
ATTN_KV_DIM = ATTN_KV_HEADS * HEAD_DIM
WINDOW = 128
GDN_HEADS = 4
GDN_DK = 128
GDN_QK_DIM = GDN_HEADS * GDN_DK
GDN_V_DIM = GDN_HEADS * GDN_DK
GDN_QKV_DIM = 2 * GDN_QK_DIM + GDN_V_DIM
N_UNITS = 2 * GDN_HEADS
LANES = 128
SUBLANES = 8
CHUNK = 128
SCAN_ROWS = 4 * CHUNK
LOG2_E = 1.4426950408889634
Q_SCALE = HEAD_DIM ** -0.5 * LOG2_E
ATTN_ROWS = 8 * WINDOW
FFN_PARTS = 2
COL_Q = 0
COL_K = COL_Q + ATTN_Q_DIM
COL_V = COL_K + ATTN_KV_DIM
COL_QKV = COL_V + ATTN_KV_DIM
COL_Z = COL_QKV + GDN_QKV_DIM
COL_AB = COL_Z + GDN_V_DIM
HY_IN_PAD = COL_AB + LANES
KVM_WIDTH = 8 * LANES
VMEM_LIMIT = 56 * 1024 * 1024

_NT = (((1,), (1,)), ((), ()))


def _sigmoid(x):
    return 1.0 / (1.0 + jnp.exp(-x))


def _silu(x):
    return x * _sigmoid(x)


def _rms(x):
    return x * lax.rsqrt(jnp.mean(x * x, axis=-1, keepdims=True) + EPS)


def _dot(a, b):
    return jnp.dot(a.astype(BF16), b.astype(BF16), preferred_element_type=F32)


def _params(*sem):
    return pltpu.CompilerParams(dimension_semantics=sem, vmem_limit_bytes=VMEM_LIMIT)


def _resident(shape):
    zeros = (0,) * len(shape)
    return pl.BlockSpec(shape, lambda *_: zeros, pipeline_mode=pl.Buffered(1))


def _ada_kernel(cond_ref, w_ref, b_ref, out_ref):
    s = _silu(cond_ref[...])
    out_ref[0] = jnp.dot(s, w_ref[0], precision=lax.Precision.HIGHEST,
                         preferred_element_type=F32) + b_ref[0]


def _ada(cond, ada_w, ada_b, tn=1536):
    depth, d, n = ada_w.shape
    rows = cond.shape[0]
    return pl.pallas_call(
        _ada_kernel,
        grid=(depth, n // tn),
        in_specs=[pl.BlockSpec((rows, d), lambda l, j: (0, 0)),
                  pl.BlockSpec((1, d, tn), lambda l, j: (l, 0, j)),
                  pl.BlockSpec((1, 1, tn), lambda l, j: (l, 0, j))],
        out_specs=pl.BlockSpec((1, rows, tn), lambda l, j: (l, 0, j)),
        out_shape=jax.ShapeDtypeStruct((depth, rows, n), F32),
        compiler_params=_params("arbitrary", "arbitrary"),
        name="ada",
    )(cond, ada_w, ada_b.reshape(depth, 1, n))


def _modulated_norm(x, gain, mod, shift_row):
    return _rms(x) * gain * (1.0 + mod[shift_row + 1:shift_row + 2]) + mod[shift_row:shift_row + 1]


def _inproj_hy_kernel(x_ref, xp_ref, xn_ref, mod_ref, g_ref, w_ref, cos_ref, sin_ref, cw_ref, par_ref,
                      q_out, kvm_out, qn_out, kn_out, vv_out, z_out, gb_out, gr_out, *, n_tiles):
    i = pl.program_id(1)
    n_parts = FFN_PARTS if x_ref.shape[1] % (FFN_PARTS * CHUNK) == 0 else 1
    tm = x_ref.shape[1] // n_parts
    ext = tm + 2 * SUBLANES
    x_ext = jnp.concatenate([xp_ref[0], x_ref[0], xn_ref[0]], axis=0)
    h_ext = _modulated_norm(x_ext, g_ref[...], mod_ref[0], 0)
    p_parts = [_dot(h_ext[r * tm:r * tm + ext], w_ref[...]) for r in range(n_parts)]
    for r, p_ext in enumerate(p_parts):
        _inproj_hy_finish(p_ext, slice(r * tm, (r + 1) * tm),
                          i > 0 if r == 0 else True, i < n_tiles - 1 if r == n_parts - 1 else True,
                          cos_ref, sin_ref, cw_ref, par_ref,
                          q_out, kvm_out, qn_out, kn_out, vv_out, z_out, gb_out, gr_out)


def _inproj_hy_finish(p_ext, rows, has_prev, has_next, cos_ref, sin_ref, cw_ref, par_ref,
                      q_out, kvm_out, qn_out, kn_out, vv_out, z_out, gb_out, gr_out):
    ext = p_ext.shape[0]
    tm = ext - 2 * SUBLANES
    main = slice(SUBLANES, SUBLANES + tm)
    p = p_ext[main]

    cos = cos_ref[rows, :]
    sin = sin_ref[rows, :]
    lane = lax.broadcasted_iota(jnp.int32, (tm, LANES), 1)
    first_half = (lane & (HEAD_DIM - 1)) < HEAD_DIM // 2
    left = lane < HEAD_DIM

    def rope(t):
        partner = jnp.where(first_half, pltpu.roll(t, LANES - HEAD_DIM // 2, 1),
                            pltpu.roll(t, HEAD_DIM // 2, 1))
        return t * cos + partner * sin

    for j in range(ATTN_Q_DIM // LANES):
        c0 = COL_Q + j * LANES
        q_out[0, rows, j * LANES:(j + 1) * LANES] = (rope(p[:, c0:c0 + LANES]) * Q_SCALE).astype(BF16)

    def masked_layouts(t):
        sw = pltpu.roll(t, HEAD_DIM, 1)
        return (jnp.where(left, t, 0.0), jnp.where(left, 0.0, sw),
                jnp.where(left, sw, 0.0), jnp.where(left, 0.0, t))

    k_lay = masked_layouts(rope(p[:, COL_K:COL_K + LANES]))
    v_lay = masked_layouts(p[:, COL_V:COL_V + LANES])
    for j, t in enumerate(k_lay + v_lay):
        kvm_out[0, rows, j * LANES:(j + 1) * LANES] = t.astype(BF16)
    z_out[0, rows, :] = p[:, COL_Z:COL_AB]

    for j in range(GDN_QKV_DIM // LANES):
        c0 = COL_QKV + j * LANES
        cols = slice(j * LANES, (j + 1) * LANES)
        t = jnp.concatenate([jnp.where(has_prev, p_ext[0:SUBLANES, c0:c0 + LANES], 0.0),
                             p[:, c0:c0 + LANES],
                             jnp.where(has_next, p_ext[SUBLANES + tm:ext, c0:c0 + LANES], 0.0)], axis=0)
        taps = cw_ref[:, cols] * 0.5
        h = (taps[0:1] * pltpu.roll(t, 1, 0) + taps[1:2] * t + taps[2:3] * pltpu.roll(t, ext - 1, 0))[main]
        y = h + h * jnp.tanh(h)
        if j < GDN_HEADS:
            y = y * lax.rsqrt(jnp.sum(y * y, axis=-1, keepdims=True) + EPS) * GDN_DK ** -0.5
            qn_out[0, rows, cols] = y
        elif j < 2 * GDN_HEADS:
            y = y * lax.rsqrt(jnp.sum(y * y, axis=-1, keepdims=True) + EPS)
            kn_out[0, rows, j * LANES - GDN_QK_DIM:(j + 1) * LANES - GDN_QK_DIM] = y
        else:
            vv_out[0, rows, j * LANES - 2 * GDN_QK_DIM:(j + 1) * LANES - 2 * GDN_QK_DIM] = y

    ab = p[:, COL_AB:HY_IN_PAD]
    t = ab + par_ref[1:2]
    softplus = jnp.maximum(t, 0.0) + jnp.log1p(jnp.exp(-jnp.abs(t)))
    g = -jnp.exp(par_ref[0:1]) * softplus
    beta = _sigmoid(ab)
    pos = lax.broadcasted_iota(jnp.int32, (tm, 1), 0) & (CHUNK - 1)
    pre = g
    suf = g
    s = 1
    while s < CHUNK:
        pre = pre + jnp.where(pos >= s, pltpu.roll(pre, s, 0), 0.0)
        suf = suf + jnp.where(pos < CHUNK - s, pltpu.roll(suf, tm - s, 0), 0.0)
        s *= 2
    gb = jnp.where(lane < GDN_HEADS, pre, jnp.where(lane < N_UNITS, suf, beta))
    gb_out[0, rows, :] = gb
    gr_out[0, :, rows] = gb.T[0:2 * N_UNITS]


def _inproj_hy(x, mod, mod_row, gain, w_pad, cos, sin, conv_w, a_log, dt_bias, tm):
    b, l, d = x.shape
    n = l // tm
    row_of = (lambda bi: bi) if mod_row is None else (lambda bi: mod_row)
    seq = lambda width: pl.BlockSpec((1, tm, width), lambda bi, i: (bi, i, 0))
    prev, nxt = _halo_specs(tm, d, l)
    par = jnp.zeros((SUBLANES, LANES), F32)
    par = par.at[0, :N_UNITS].set(a_log.reshape(-1)).at[1, :N_UNITS].set(dt_bias.reshape(-1))
    return pl.pallas_call(
        functools.partial(_inproj_hy_kernel, n_tiles=n),
        grid=(b, n),
        in_specs=[seq(d), prev, nxt,
                  pl.BlockSpec((1, 6, d), lambda bi, i: (row_of(bi), 0, 0)),
                  _resident((1, d)),
                  _resident((d, HY_IN_PAD)),
                  pl.BlockSpec((tm, LANES), lambda bi, i: (i, 0)),
                  pl.BlockSpec((tm, LANES), lambda bi, i: (i, 0)),
                  _resident((3, GDN_QKV_DIM)), _resident((SUBLANES, LANES))],
        out_specs=[seq(ATTN_Q_DIM), seq(KVM_WIDTH), seq(GDN_QK_DIM), seq(GDN_QK_DIM), seq(GDN_V_DIM),
                   seq(GDN_V_DIM), seq(LANES),
                   pl.BlockSpec((1, 2 * N_UNITS, tm), lambda bi, i: (bi, 0, i))],
        out_shape=[jax.ShapeDtypeStruct((b, l, ATTN_Q_DIM), BF16),
                   jax.ShapeDtypeStruct((b, l, KVM_WIDTH), BF16),
                   jax.ShapeDtypeStruct((b, l, GDN_QK_DIM), F32),
                   jax.ShapeDtypeStruct((b, l, GDN_QK_DIM), F32),
                   jax.ShapeDtypeStruct((b, l, GDN_V_DIM), F32),
                   jax.ShapeDtypeStruct((b, l, GDN_V_DIM), F32),
                   jax.ShapeDtypeStruct((b, l, LANES), F32),
                   jax.ShapeDtypeStruct((b, 2 * N_UNITS, l), F32)],
        compiler_params=_params("parallel", "arbitrary"),
        name="inproj_hy",
    )(x, x, x, mod, gain.reshape(1, d), w_pad, cos, sin, conv_w, par)


def _conv3_rows(x, prev_row, next_row, w, row):
    tm = x.shape[0]
    xp = jnp.where(row == 0, prev_row, pltpu.roll(x, 1, 0))
    xn = jnp.where(row == tm - 1, next_row, pltpu.roll(x, tm - 1, 0))
    return w[0:1] * xp + w[1:2] * x + w[2:3] * xn


def _halo_specs(tm, width, n_rows):
    per = tm // SUBLANES
    last = n_rows // SUBLANES - 1
    prev = pl.BlockSpec((1, SUBLANES, width), lambda bi, i: (bi, jnp.maximum(i * per - 1, 0), 0))
    nxt = pl.BlockSpec((1, SUBLANES, width), lambda bi, i: (bi, jnp.minimum((i + 1) * per, last), 0))
    return prev, nxt


def _unit_triangular_inverses(a_list, eye, diag_blocks, level_masks):
    ad = [jnp.where(diag_blocks, a, 0.0) for a in a_list]
    a2 = [_dot(x, x) for x in ad]
    a4 = [_dot(x, x) for x in a2]
    t = [_dot(eye - x, eye + y) for x, y in zip(ad, a2)]
    a8 = [_dot(x, x) for x in a4]
    t = [_dot(x, eye + y) for x, y in zip(t, a4)]
    t = [_dot(x, eye + y) for x, y in zip(t, a8)]
    for m in level_masks:
        at = [_dot(jnp.where(m, a, 0.0), x) for a, x in zip(a_list, t)]
        t = [x - _dot(x, y) for x, y in zip(t, at)]
    return t


def _gdn_scan_kernel(qf_ref, kf_ref, vf_ref, gbf_ref, grf_ref,
                     qb_ref, kb_ref, vb_ref, gbb_ref, grb_ref, s0_ref,
                     of_ref, ob_ref, s_out, s_scr, *, n_blocks):
    i = pl.program_id(1)

    @pl.when(i == 0)
    def _():
        s_scr[...] = s0_ref[0]

    row = lax.broadcasted_iota(jnp.int32, (CHUNK, CHUNK), 0)
    col = lax.broadcasted_iota(jnp.int32, (CHUNK, CHUNK), 1)
    eye = jnp.where(row == col, 1.0, 0.0)
    same = lambda shift: (row >> shift) == (col >> shift)
    diag_blocks = same(4)
    level_masks = [same(sh) & jnp.logical_not(same(sh - 1)) for sh in (5, 6, 7)]

    units = []
    for d, refs in enumerate(((qf_ref, kf_ref, vf_ref, gbf_ref, grf_ref, of_ref),
                              (qb_ref, kb_ref, vb_ref, gbb_ref, grb_ref, ob_ref))):
        for h in range(GDN_HEADS):
            units.append((d, h, d * GDN_HEADS + h, slice(h * GDN_DK, (h + 1) * GDN_DK)) + refs)

    n_chunks = qf_ref.shape[1] // CHUNK
    q, k, v, g_col, beta, g_last, decay, kq, a = ({} for _ in range(9))
    for c in range(n_chunks):
        rows = slice(c * CHUNK, (c + 1) * CHUNK)
        for d, h, u, cols, q_ref, k_ref, v_ref, gb_ref, gr_ref, o_ref in units:
            incl = (row >= col) if d == 0 else (row <= col)
            last = CHUNK - 1 if d == 0 else 0
            key = (c, u)
            q[key] = q_ref[0, rows, cols]
            k[key] = k_ref[0, rows, cols]
            v[key] = v_ref[0, rows, cols]
            g_col[key] = gb_ref[0, rows, u:u + 1]
            beta[key] = gb_ref[0, rows, N_UNITS + u:N_UNITS + u + 1]
            g_last[key] = g_col[key][last:last + 1, :]
            decay[key] = jnp.exp(jnp.where(incl, g_col[key] - gr_ref[0, u:u + 1, rows], -jnp.inf))
            kq[key] = lax.dot_general(jnp.concatenate([k[key], q[key]], axis=0).astype(BF16),
                                      k[key].astype(BF16), _NT, preferred_element_type=F32)
    keys = list(kq)
    for key in keys:
        strict = (row > col) if key[1] < GDN_HEADS else (row < col)
        a[key] = jnp.where(strict, beta[key] * kq[key][:CHUNK] * decay[key], 0.0)
    t_mat = dict(zip(keys, _unit_triangular_inverses([a[key] for key in keys], eye, diag_blocks, level_masks)))

    state = [s_scr[u] for u in range(N_UNITS)]
    for step in range(n_chunks):
        order = [(step if unit[0] == 0 else n_chunks - 1 - step, unit[2]) for unit in units]
        ps = [_dot(jnp.concatenate([k[key] * jnp.exp(g_col[key]), q[key] * jnp.exp(g_col[key])], axis=0),
                   state[key[1]]) for key in order]
        v_new = [_dot(t_mat[key], beta[key] * (v[key] - ps_u[:CHUNK])) for key, ps_u in zip(order, ps)]
        for unit, key, ps_u, v_new_u in zip(units, order, ps, v_new):
            cols, o_ref = unit[3], unit[9]
            o_ref[0, key[0] * CHUNK:(key[0] + 1) * CHUNK, cols] = (
                ps_u[CHUNK:] + _dot(kq[key][CHUNK:] * decay[key], v_new_u))
        for key, v_new_u in zip(order, v_new):
            k_tail = k[key] * jnp.exp(g_last[key] - g_col[key])
            state[key[1]] = state[key[1]] * jnp.exp(g_last[key]) + _dot(k_tail.T, v_new_u)
    for u in range(N_UNITS):
        s_scr[u] = state[u]

    @pl.when(i == n_blocks - 1)
    def _():
        s_out[0] = s_scr[...]


def _gdn_scan(qn, kn, vv, gb, gr, state0, rows=SCAN_ROWS):
    b, l, _ = qn.shape
    n = l // rows
    fwd = lambda width: pl.BlockSpec((1, rows, width), lambda bi, i: (bi, i, 0))
    bwd = lambda width: pl.BlockSpec((1, rows, width), lambda bi, i: (bi, n - 1 - i, 0))
    gr_f = pl.BlockSpec((1, 2 * N_UNITS, rows), lambda bi, i: (bi, 0, i))
    gr_b = pl.BlockSpec((1, 2 * N_UNITS, rows), lambda bi, i: (bi, 0, n - 1 - i))
    st = pl.BlockSpec((1, N_UNITS, GDN_DK, GDN_DK), lambda bi, i: (bi, 0, 0, 0))
    return pl.pallas_call(
        functools.partial(_gdn_scan_kernel, n_blocks=n),
        grid=(b, n),
        in_specs=[fwd(GDN_QK_DIM), fwd(GDN_QK_DIM), fwd(GDN_V_DIM), fwd(LANES), gr_f,
                  bwd(GDN_QK_DIM), bwd(GDN_QK_DIM), bwd(GDN_V_DIM), bwd(LANES), gr_b, st],
        out_specs=[fwd(GDN_V_DIM), bwd(GDN_V_DIM), st],
        out_shape=[jax.ShapeDtypeStruct((b, l, GDN_V_DIM), F32),
                   jax.ShapeDtypeStruct((b, l, GDN_V_DIM), F32),
                   jax.ShapeDtypeStruct((b, N_UNITS, GDN_DK, GDN_DK), F32)],
        scratch_shapes=[pltpu.VMEM((N_UNITS, GDN_DK, GDN_DK), F32)],
        compiler_params=_params("parallel", "arbitrary"),
        name="gdn_scan",
    )(qn, kn, vv, gb, gr, qn, kn, vv, gb, gr, state0)


def _attn_kernel(q_ref, kvp_ref, kvc_ref, kvn_ref, ctx_ref, sink_ref, o_ref, *, n_steps, n_ctx_tiles):
    i = pl.program_id(1)
    n_sub = q_ref.shape[1] // WINDOW
    row = lax.broadcasted_iota(jnp.int32, (WINDOW, WINDOW), 0)
    col = lax.broadcasted_iota(jnp.int32, (WINDOW, WINDOW), 1)
    n_pairs = ATTN_Q_HEADS // 2
    left = lax.broadcasted_iota(jnp.int32, (WINDOW, LANES), 1) < HEAD_DIM
    for sub in range(n_sub):
        _attn_block(sub, i, n_sub, n_steps, n_ctx_tiles, row, col, n_pairs, left,
                    q_ref, kvp_ref, kvc_ref, kvn_ref, ctx_ref, sink_ref, o_ref)


def _attn_block(sub, i, n_sub, n_steps, n_ctx_tiles, row, col, n_pairs, left,
                q_ref, kvp_ref, kvc_ref, kvn_ref, ctx_ref, sink_ref, o_ref):
    rows = slice(sub * WINDOW, (sub + 1) * WINDOW)
    mask_prev = (col >= row) & (i > 0) if sub == 0 else (col >= row)
    mask_next = (col <= row) & (i < n_steps - 1) if sub == n_sub - 1 else (col <= row)

    def tiles(head, base):
        kv_head = head // (ATTN_Q_HEADS // ATTN_KV_HEADS)
        c = (base + 2 * kv_head + head % 2) * LANES
        local = ([kvp_ref[0, :, c:c + LANES]]
                 + [kvc_ref[0, t * WINDOW:(t + 1) * WINDOW, c:c + LANES] for t in range(n_sub)]
                 + [kvn_ref[0, :, c:c + LANES]])
        return local[sub:sub + 3] + [ctx_ref[0, t * WINDOW:(t + 1) * WINDOW, c:c + LANES]
                                     for t in range(n_ctx_tiles)]

    def pair_scores(pair):
        q = q_ref[0, rows, pair * LANES:(pair + 1) * LANES]
        both = [lax.dot_general(q, jnp.concatenate([ka, kb], axis=0), _NT, preferred_element_type=F32)
                for ka, kb in zip(tiles(2 * pair, 0), tiles(2 * pair + 1, 0))]
        return [t[:, :WINDOW] for t in both], [t[:, WINDOW:] for t in both]

    def softmax_terms(head, s):
        sink = sink_ref[head] * LOG2_E
        s = [jnp.where(mask_prev, s[0], -jnp.inf), s[1], jnp.where(mask_next, s[2], -jnp.inf)] + s[3:]
        m = s[1]
        for st in s[:1] + s[2:]:
            m = jnp.maximum(m, st)
        m = jnp.maximum(jnp.max(m, axis=-1, keepdims=True), sink)
        e = [jnp.exp2(st - m) for st in s]
        tot = e[0]
        for et in e[1:]:
            tot = tot + et
        inv_denom = 1.0 / (jnp.sum(tot, axis=-1, keepdims=True) + jnp.exp2(sink - m))
        return [et.astype(BF16) for et in e], inv_denom

    def pair_output(pair, terms_a, terms_b):
        pv = None
        for ea, eb, va, vb in zip(terms_a[0], terms_b[0], tiles(2 * pair, 4), tiles(2 * pair + 1, 4)):
            part = jnp.dot(jnp.concatenate([ea, eb], axis=1), jnp.concatenate([va, vb], axis=0),
                           preferred_element_type=F32)
            pv = part if pv is None else pv + part
        acc = pv * jnp.where(left, terms_a[1], terms_b[1])
        o_ref[0, rows, pair * LANES:(pair + 1) * LANES] = acc.astype(BF16)

    scores = [pair_scores(pair) for pair in range(n_pairs)]
    terms = [(softmax_terms(2 * pair, s[0]), softmax_terms(2 * pair + 1, s[1])) for pair, s in enumerate(scores)]
    for pair, (terms_a, terms_b) in enumerate(terms):
        pair_output(pair, terms_a, terms_b)


def _attn(q, kvm, kvm_ctx, sink):
    b, l, _ = q.shape
    lc = kvm_ctx.shape[1]
    rows = _row_tile(l, ATTN_ROWS)
    n = l // rows
    per = rows // WINDOW
    last = l // WINDOW - 1
    blk = lambda f: pl.BlockSpec((1, WINDOW, KVM_WIDTH), f)
    return pl.pallas_call(
        functools.partial(_attn_kernel, n_steps=n, n_ctx_tiles=lc // WINDOW),
        grid=(b, n),
        in_specs=[pl.BlockSpec((1, rows, ATTN_Q_DIM), lambda bi, i: (bi, i, 0)),
                  blk(lambda bi, i: (bi, jnp.maximum(i * per - 1, 0), 0)),
                  pl.BlockSpec((1, rows, KVM_WIDTH), lambda bi, i: (bi, i, 0)),
                  blk(lambda bi, i: (bi, jnp.minimum((i + 1) * per, last), 0)),
                  pl.BlockSpec((1, lc, KVM_WIDTH), lambda bi, i: (bi, 0, 0)),
                  pl.BlockSpec(memory_space=pltpu.SMEM)],
        out_specs=pl.BlockSpec((1, rows, ATTN_Q_DIM), lambda bi, i: (bi, i, 0)),
        out_shape=jax.ShapeDtypeStruct((b, l, ATTN_Q_DIM), BF16),
        compiler_params=_params("parallel", "arbitrary"),
        name="attn",
    )(q, kvm, kvm, kvm, kvm_ctx, sink)


def _row_parts(tm):
    part = tm // FFN_PARTS
    return [slice(r * part, (r + 1) * part) for r in range(FFN_PARTS)]


def _residual_ffn(xs, ys, mod, gains, wg_ref, wu_ref, wd_ref):
    x1 = [x + mod[2:3] * (_rms(y) * gains[0:1]) for x, y in zip(xs, ys)]
    h = [_modulated_norm(t, gains[1:2], mod, 3).astype(BF16) for t in x1]
    gate_up = [(jnp.dot(t, wg_ref[...], preferred_element_type=F32),
                jnp.dot(t, wu_ref[...], preferred_element_type=F32)) for t in h]
    f = [_dot(_silu(gate) * up, wd_ref[...]) for gate, up in gate_up]
    return [t + mod[5:6] * (_rms(ft) * gains[2:3]) for t, ft in zip(x1, f)]


def _post_hy_kernel(x_ref, attn_ref, of_ref, ob_ref, z_ref, mod_ref, gains_ref, ng_ref,
                    wo_ref, wg_ref, wu_ref, wd_ref, out_ref):
    parts = _row_parts(x_ref.shape[1])
    ys = []
    for rows in parts:
        y = jnp.dot(attn_ref[0, rows, :], wo_ref[0:ATTN_Q_DIM, :], preferred_element_type=F32)
        for h in range(GDN_HEADS):
            cols = slice(h * GDN_DK, (h + 1) * GDN_DK)
            gated = (_rms(of_ref[0, rows, cols] + ob_ref[0, rows, cols]) * ng_ref[...]
                     * _silu(z_ref[0, rows, cols]))
            y = y + _dot(gated, wo_ref[ATTN_Q_DIM + h * GDN_DK:ATTN_Q_DIM + (h + 1) * GDN_DK, :])
        ys.append(y)
    outs = _residual_ffn([x_ref[0, rows, :] for rows in parts], ys, mod_ref[0], gains_ref[...],
                         wg_ref, wu_ref, wd_ref)
    for rows, out in zip(parts, outs):
        out_ref[0, rows, :] = out


def _post_hy(x, attn, o_f, o_b, z, mod, gains, norm_g, w_out, w_gate, w_up, w_down, tm):
    b, l, d = x.shape
    hidden = w_gate.shape[1]
    seq = lambda width: pl.BlockSpec((1, tm, width), lambda bi, i: (bi, i, 0))
    return pl.pallas_call(
        _post_hy_kernel,
        grid=(b, l // tm),
        in_specs=[seq(d), seq(ATTN_Q_DIM), seq(GDN_V_DIM), seq(GDN_V_DIM), seq(GDN_V_DIM),
                  pl.BlockSpec((1, 6, d), lambda bi, i: (bi, 0, 0)),
                  _resident((3, d)), _resident((1, GDN_DK)),
                  _resident((ATTN_Q_DIM + GDN_V_DIM, d)),
                  _resident((d, hidden)), _resident((d, hidden)), _resident((hidden, d))],
        out_specs=seq(d),
        out_shape=jax.ShapeDtypeStruct((b, l, d), F32),
        compiler_params=_params("parallel", "arbitrary"),
        name="post_hy",
    )(x, attn, o_f, o_b, z, mod, gains, norm_g.reshape(1, GDN_DK), w_out, w_gate, w_up, w_down)


def _inproj_sc_kernel(x_ref, mod_ref, g_ref, w_ref, b_out, cu_out):
    d = x_ref.shape[2]
    parts = _row_parts(x_ref.shape[1])
    hs = [_modulated_norm(x_ref[0, rows, :], g_ref[...], mod_ref[0], 0) for rows in parts]
    ps = [_dot(h, w_ref[...]) for h in hs]
    for rows, p in zip(parts, ps):
        b_out[0, rows, :] = p[:, 0:d]
        cu_out[0, rows, :] = p[:, d:2 * d] * p[:, 2 * d:3 * d]


def _inproj_sc(x, mod, gain, w_in, tm):
    b, l, d = x.shape
    seq = pl.BlockSpec((1, tm, d), lambda bi, i: (bi, i, 0))
    return pl.pallas_call(
        _inproj_sc_kernel,
        grid=(b, l // tm),
        in_specs=[seq, pl.BlockSpec((1, 6, d), lambda bi, i: (bi, 0, 0)),
                  _resident((1, d)), _resident((d, 3 * d))],
        out_specs=[seq, seq],
        out_shape=[jax.ShapeDtypeStruct((b, l, d), F32), jax.ShapeDtypeStruct((b, l, d), F32)],
        compiler_params=_params("parallel", "arbitrary"),
        name="inproj_sc",
    )(x, mod, gain.reshape(1, d), w_in)


def _post_sc_kernel(x_ref, b_ref, cu_ref, prev_ref, next_ref, mod_ref, gains_ref, cw_ref,
                    wo_ref, wg_ref, wu_ref, wd_ref, out_ref, *, n_tiles):
    i = pl.program_id(1)
    tm = x_ref.shape[1]
    row = lax.broadcasted_iota(jnp.int32, (tm, 1), 0)
    prev_row = jnp.where(i > 0, prev_ref[0, SUBLANES - 1:SUBLANES, :], 0.0)
    next_row = jnp.where(i < n_tiles - 1, next_ref[0, 0:1, :], 0.0)
    conv = _conv3_rows(cu_ref[0], prev_row, next_row, cw_ref[...], row)
    parts = _row_parts(tm)
    ys = [_dot(b_ref[0, rows, :] * conv[rows], wo_ref[...]) for rows in parts]
    outs = _residual_ffn([x_ref[0, rows, :] for rows in parts], ys, mod_ref[0], gains_ref[...],
                         wg_ref, wu_ref, wd_ref)
    for rows, out in zip(parts, outs):
        out_ref[0, rows, :] = out


def _post_sc(x, b_gate, cu, mod, gains, conv_w, w_out, w_gate, w_up, w_down, tm):
    b, l, d = x.shape
    hidden = w_gate.shape[1]
    n = l // tm
    seq = pl.BlockSpec((1, tm, d), lambda bi, i: (bi, i, 0))
    prev, nxt = _halo_specs(tm, d, l)
    return pl.pallas_call(
        functools.partial(_post_sc_kernel, n_tiles=n),
        grid=(b, n),
        in_specs=[seq, seq, seq, prev, nxt,
                  pl.BlockSpec((1, 6, d), lambda bi, i: (bi, 0, 0)),
                  _resident((3, d)), _resident((3, d)), _resident((d, d)),
                  _resident((d, hidden)), _resident((d, hidden)), _resident((hidden, d))],
        out_specs=seq,
        out_shape=jax.ShapeDtypeStruct((b, l, d), F32),
        compiler_params=_params("parallel", "arbitrary"),
        name="post_sc",
    )(x, b_gate, cu, cu, cu, mod, gains, conv_w, w_out, w_gate, w_up, w_down)


def _rope_tables(length):
    pos = jnp.arange(length)
    n_freq = HEAD_DIM // 4
    inv_freq = ROPE_BASE ** (-jnp.arange(n_freq, dtype=F32) / n_freq)
    ang = jnp.concatenate([(pos // GRID_W).astype(F32)[:, None] * inv_freq,
                           (pos % GRID_W).astype(F32)[:, None] * inv_freq], axis=-1)
    cos, sin = jnp.cos(ang), jnp.sin(ang)
    return jnp.tile(cos, (1, 4)), jnp.tile(jnp.concatenate([-sin, sin], axis=-1), (1, 2))


def _row_tile(length, target):
    tm = min(length, target)
    assert length % tm == 0 and tm % CHUNK == 0
    return tm


def kernel(x, c, ctx, c_ctx, ada_w, ada_b, pre_mix_g, post_mix_g, pre_ffn_g, post_ffn_g, hy_w_in, hy_w_out,
           attn_sink, gdn_conv_w, gdn_a_log, gdn_dt_bias, gdn_norm_g, sc_w_in, sc_conv_w, sc_w_out,
           ffn_w_gate, ffn_w_up, ffn_w_down):
    b, l, d = x.shape
    lc = ctx.shape[1]
    assert ada_w.shape[0] == 2 and l % CHUNK == 0 and lc % CHUNK == 0 and b + 1 <= SUBLANES
    tm = _row_tile(l, 512)
    tmc = _row_tile(lc, 256)

    cond = jnp.concatenate([c, c_ctx[None], jnp.zeros((SUBLANES - b - 1, d), F32)], axis=0)
    mod = _ada(cond, ada_w, ada_b).reshape(2, SUBLANES, 6, d)
    gains = jnp.stack([post_mix_g, pre_ffn_g, post_ffn_g], axis=1)
    bf = lambda w: w.astype(BF16)

    w_in = bf(jnp.pad(hy_w_in[0], ((0, 0), (0, HY_IN_PAD - hy_w_in.shape[2]))))
    cos, sin = _rope_tables(l)
    gdn_par = (gdn_conv_w[0], gdn_a_log[0], gdn_dt_bias[0])
    q, kvm, qn, kn, vv, z, gb, gr = _inproj_hy(x, mod[0], None, pre_mix_g[0], w_in, cos, sin, *gdn_par, tm)
    ones, zeros = jnp.ones((lc, LANES), F32), jnp.zeros((lc, LANES), F32)
    _, kvm_c, qn_c, kn_c, vv_c, _, gb_c, gr_c = _inproj_hy(ctx, mod[0], b, pre_mix_g[0], w_in, ones, zeros,
                                                          *gdn_par, tmc)

    state0 = jnp.zeros((b, N_UNITS, GDN_DK, GDN_DK), F32)
    _, _, state_c = _gdn_scan(qn_c, kn_c, vv_c, gb_c, gr_c, state0, _row_tile(lc, SCAN_ROWS))
    o_f, o_b, _ = _gdn_scan(qn, kn, vv, gb, gr, state_c, _row_tile(l, SCAN_ROWS))

    attn = _attn(q, kvm, kvm_c, attn_sink[0])
    x = _post_hy(x, attn, o_f, o_b, z, mod[0], gains[0], gdn_norm_g[0], bf(hy_w_out[0]),
                 bf(ffn_w_gate[0]), bf(ffn_w_up[0]), bf(ffn_w_down[0]), tm)

    b_gate, cu = _inproj_sc(x, mod[1], pre_mix_g[1], bf(sc_w_in[0]), _row_tile(l, 2 * tm))
    return _post_sc(x, b_gate, cu, mod[1], gains[1], sc_conv_w[0], bf(sc_w_out[0]),
                    bf(ffn_w_gate[1]), bf(ffn_w_up[1]), bf(ffn_w_down[1]), tm)
```
